```python
import math
import jax, jax.numpy as jnp
from jax import lax
import numpy as np

D_MODEL = 2048
BATCH = 4
SEQ = 4096
DEPTH = 2

N_MIXERS = 2
N_A_LAYERS = (DEPTH + 1) // 2
N_B_LAYERS = DEPTH // 2
BLOCK_Q = 128
ROPE_THETA = 10000.0

DA_HEAD_DIM = 128
DA_HEADS = D_MODEL // (2 * DA_HEAD_DIM)
DA_V_DIM = 2 * DA_HEAD_DIM
DA_QK_WIDTH = DA_HEADS * 2 * DA_HEAD_DIM
DA_V_WIDTH = DA_HEADS * DA_V_DIM
DA_IN = 2 * DA_QK_WIDTH + DA_V_WIDTH

FX_HEAD_DIM = 128
FX_HEADS = D_MODEL // FX_HEAD_DIM
FX_WIDTH = FX_HEADS * FX_HEAD_DIM
FX_IN = 3 * FX_WIDTH + FX_HEADS
FORGET_BIAS_INIT = 3.0

N_GROUPS = 4
EXPERTS_PER_GROUP = 8
EXPERT_DFF = D_MODEL // 4
TOP_K_INNER = 2

LN_EPS = 1e-5
RMS_EPS = 1e-5
DEEPNORM_ALPHA = (2.0 * DEPTH) ** 0.25
DEEPNORM_BETA = (8.0 * DEPTH) ** -0.25

kernel_name = "hybrid_diffattn_fox_hmoe_deepnorm_adaln"


def layer_norm(x, g, b):
    x32 = x.astype(jnp.float32)
    mu = jnp.mean(x32, axis=-1, keepdims=True)
    var = jnp.mean(jnp.square(x32 - mu), axis=-1, keepdims=True)
    y = (x32 - mu) * lax.rsqrt(var + LN_EPS)
    return (y * g.astype(jnp.float32) + b.astype(jnp.float32)).astype(x.dtype)


def rms_norm(x, g):
    x32 = x.astype(jnp.float32)
    y = x32 * lax.rsqrt(jnp.mean(jnp.square(x32), axis=-1, keepdims=True) + RMS_EPS)
    return (y * g.astype(jnp.float32)).astype(x.dtype)


def adaln_modulation(c, w_mod, b_mod):
    m = jax.nn.silu(c) @ w_mod + b_mod
    shift, scale, g = jnp.split(m, 3, axis=-1)
    return shift[:, None, :], scale[:, None, :], 1.0 + g[:, None, :]


def apply_rope(x, positions):
    d = x.shape[-1]
    inv_freq = jnp.power(ROPE_THETA, -jnp.arange(0, d, 2, dtype=jnp.float32) / d)
    ang = positions.astype(jnp.float32)[:, None] * inv_freq[None, :]
    cos = jnp.cos(ang)[:, None, :]
    sin = jnp.sin(ang)[:, None, :]
    x32 = x.astype(jnp.float32)
    x1, x2 = jnp.split(x32, 2, axis=-1)
    out = jnp.concatenate([x1 * cos - x2 * sin, x2 * cos + x1 * sin], axis=-1)
    return out.astype(x.dtype)


def causal_mask(i, length):
    q_pos = i * BLOCK_Q + jnp.arange(BLOCK_Q)
    k_pos = jnp.arange(length)
    return q_pos[:, None] >= k_pos[None, :]


def diff_lambda_init(layer_idx):
    return 0.8 - 0.6 * math.exp(-0.3 * (layer_idx - 1))


def differential_attention(h, positions, w_in, lam_q1, lam_k1, lam_q2, lam_k2, subln_w, w_out, lam_init):
    B, S, _ = h.shape
    proj = h @ w_in
    q, k, v = jnp.split(proj, [DA_QK_WIDTH, 2 * DA_QK_WIDTH], axis=-1)
    q = apply_rope(q.reshape(B, S, 2 * DA_HEADS, DA_HEAD_DIM), positions)
    k = apply_rope(k.reshape(B, S, 2 * DA_HEADS, DA_HEAD_DIM), positions)
    q = q.reshape(B, S, DA_HEADS, 2, DA_HEAD_DIM).transpose(0, 2, 3, 1, 4)
    k = k.reshape(B, S, DA_HEADS, 2, DA_HEAD_DIM).transpose(0, 2, 3, 1, 4)
    v = v.reshape(B, S, DA_HEADS, DA_V_DIM).transpose(0, 2, 1, 3)
    f32 = jnp.float32
    lam = (jnp.exp(jnp.sum(lam_q1.astype(f32) * lam_k1.astype(f32)))
           - jnp.exp(jnp.sum(lam_q2.astype(f32) * lam_k2.astype(f32))) + lam_init)
    scale = DA_HEAD_DIM ** -0.5
    outs = []
    for i in range(S // BLOCK_Q):
        L = (i + 1) * BLOCK_Q
        q_blk = q[:, :, :, i * BLOCK_Q:L]
        s = jnp.einsum('bhcqd,bhckd->bhcqk', q_blk, k[:, :, :, :L]).astype(f32) * scale
        s = jnp.where(causal_mask(i, L), s, -jnp.inf)
        p = jax.nn.softmax(s, axis=-1)
        a = p[:, :, 0] - lam * p[:, :, 1]
        outs.append(jnp.einsum('bhqk,bhkd->bhqd', a.astype(v.dtype), v[:, :, :L]))
    o = jnp.concatenate(outs, axis=2)
    o = rms_norm(o, subln_w) * (1.0 - lam_init)
    o = o.transpose(0, 2, 1, 3).reshape(B, S, DA_V_WIDTH)
    return o @ w_out


def forgetting_attention(h, w_in, forget_bias, w_out):
    B, S, _ = h.shape
    proj = h @ w_in
    q, k, v, f_logit = jnp.split(proj, [FX_WIDTH, 2 * FX_WIDTH, 3 * FX_WIDTH], axis=-1)
    q = q.reshape(B, S, FX_HEADS, FX_HEAD_DIM).transpose(0, 2, 1, 3)
    k = k.reshape(B, S, FX_HEADS, FX_HEAD_DIM).transpose(0, 2, 1, 3)
    v = v.reshape(B, S, FX_HEADS, FX_HEAD_DIM).transpose(0, 2, 1, 3)
    f32 = jnp.float32
    log_f = jax.nn.log_sigmoid(f_logit.astype(f32) + forget_bias.astype(f32))
    cum = jnp.cumsum(log_f, axis=1).transpose(0, 2, 1)
    scale = FX_HEAD_DIM ** -0.5
    outs = []
    for i in range(S // BLOCK_Q):
        L = (i + 1) * BLOCK_Q
        q_blk = q[:, :, i * BLOCK_Q:L]
        s = jnp.einsum('bhqd,bhkd->bhqk', q_blk, k[:, :, :L]).astype(f32) * scale
        s = s + (cum[:, :, i * BLOCK_Q:L, None] - cum[:, :, None, :L])
        s = jnp.where(causal_mask(i, L), s, -jnp.inf)
        p = jax.nn.softmax(s, axis=-1)
        outs.append(jnp.einsum('bhqk,bhkd->bhqd', p.astype(v.dtype), v[:, :, :L]))
    o = jnp.concatenate(outs, axis=2)
    o = o.transpose(0, 2, 1, 3).reshape(B, S, FX_WIDTH)
    return o @ w_out


def hierarchical_moe(h, w_group, b_group, w_router, b_router, w_gate, w_up, w_down):
    B, S, D = h.shape
    t = h.reshape(B * S, D)
    f32 = jnp.float32
    g_logits = (t @ w_group + b_group).astype(f32)
    g_probs = jax.nn.softmax(g_logits, axis=-1)
    g_idx = jnp.argmax(g_logits, axis=-1)
    g_w = jnp.take_along_axis(g_probs, g_idx[:, None], axis=1)[:, 0]
    e_all = (jnp.einsum('nd,gde->nge', t, w_router) + b_router).astype(f32)
    e_logits = jnp.take_along_axis(e_all, g_idx[:, None, None], axis=1)[:, 0]
    top_v, top_i = lax.top_k(e_logits, TOP_K_INNER)
    top_p = jax.nn.softmax(top_v, axis=-1)
    combine = jnp.sum(jax.nn.one_hot(top_i, EXPERTS_PER_GROUP, dtype=f32) * top_p[..., None], axis=1)
    weights = (g_w[:, None] * combine).astype(t.dtype)
    out = jnp.zeros_like(t)
    for g in range(N_GROUPS):
        wg = jnp.where((g_idx == g)[:, None], weights, jnp.zeros_like(weights))
        a = jnp.einsum('nd,edf->nef', t, w_gate[g])
        u = jnp.einsum('nd,edf->nef', t, w_up[g])
        hid = jax.nn.silu(a) * u * wg[:, :, None]
        out = out + jnp.einsum('nef,efd->nd', hid, w_down[g])
    return out.reshape(B, S, D)


def setup_inputs(seed: int = 0) -> dict:
    key = jax.random.key(seed)
    ks = jax.random.split(key, 32)
    nrm = jax.random.normal
    D = D_MODEL
    s_d = D ** -0.5
    x = nrm(ks[0], (BATCH, SEQ, D), jnp.float32)
    c = nrm(ks[1], (BATCH, D), jnp.float32)
    mix_mod_w = nrm(ks[2], (DEPTH, D, 3 * D)) * s_d * 0.1
    mix_mod_b = nrm(ks[3], (DEPTH, 3 * D)) * 0.02
    mix_ln_g = 1.0 + 0.02 * nrm(ks[4], (DEPTH, D))
    mix_ln_b = 0.02 * nrm(ks[5], (DEPTH, D))
    a_w_qk = nrm(ks[6], (N_A_LAYERS, D, 2 * DA_QK_WIDTH)) * s_d
    a_w_v = nrm(ks[7], (N_A_LAYERS, D, DA_V_WIDTH)) * s_d * DEEPNORM_BETA
    a_w_in = jnp.concatenate([a_w_qk, a_w_v], axis=-1)
    a_lam_q1 = 0.1 * nrm(ks[8], (N_A_LAYERS, DA_HEAD_DIM))
    a_lam_k1 = 0.1 * nrm(ks[9], (N_A_LAYERS, DA_HEAD_DIM))
    a_lam_q2 = 0.1 * nrm(ks[10], (N_A_LAYERS, DA_HEAD_DIM))
    a_lam_k2 = 0.1 * nrm(ks[11], (N_A_LAYERS, DA_HEAD_DIM))
    a_subln_w = 1.0 + 0.02 * nrm(ks[12], (N_A_LAYERS, DA_V_DIM))
    a_w_out = nrm(ks[13], (N_A_LAYERS, DA_V_WIDTH, D)) * DA_V_WIDTH ** -0.5 * DEEPNORM_BETA
    b_w_qk = nrm(ks[14], (N_B_LAYERS, D, 2 * FX_WIDTH)) * s_d
    b_w_v = nrm(ks[15], (N_B_LAYERS, D, FX_WIDTH)) * s_d * DEEPNORM_BETA
    b_w_f = nrm(ks[16], (N_B_LAYERS, D, FX_HEADS)) * s_d * 0.5
    b_w_in = jnp.concatenate([b_w_qk, b_w_v, b_w_f], axis=-1)
    b_forget_bias = FORGET_BIAS_INIT + 0.5 * nrm(ks[17], (N_B_LAYERS, FX_HEADS))
    b_w_out = nrm(ks[18], (N_B_LAYERS, FX_WIDTH, D)) * FX_WIDTH ** -0.5 * DEEPNORM_BETA
    ffn_mod_w = nrm(ks[19], (DEPTH, D, 3 * D)) * s_d * 0.1
    ffn_mod_b = nrm(ks[20], (DEPTH, 3 * D)) * 0.02
    ffn_ln_g = 1.0 + 0.02 * nrm(ks[21], (DEPTH, D))
    ffn_ln_b = 0.02 * nrm(ks[22], (DEPTH, D))
    moe_w_group = nrm(ks[23], (DEPTH, D, N_GROUPS)) * s_d
    moe_b_group = 0.01 * nrm(ks[24], (DEPTH, N_GROUPS))
    moe_w_router = nrm(ks[25], (DEPTH, N_GROUPS, D, EXPERTS_PER_GROUP)) * s_d
    moe_b_router = 0.01 * nrm(ks[26], (DEPTH, N_GROUPS, EXPERTS_PER_GROUP))
    moe_w_gate = nrm(ks[27], (DEPTH, N_GROUPS, EXPERTS_PER_GROUP, D, EXPERT_DFF)) * s_d
    moe_w_up = nrm(ks[28], (DEPTH, N_GROUPS, EXPERTS_PER_GROUP, D, EXPERT_DFF)) * s_d * DEEPNORM_BETA
    moe_w_down = nrm(ks[29], (DEPTH, N_GROUPS, EXPERTS_PER_GROUP, EXPERT_DFF, D)) * EXPERT_DFF ** -0.5 * DEEPNORM_BETA
    return {
        "x": x, "c": c,
        "mix_mod_w": mix_mod_w, "mix_mod_b": mix_mod_b, "mix_ln_g": mix_ln_g, "mix_ln_b": mix_ln_b,
        "a_w_in": a_w_in, "a_lam_q1": a_lam_q1, "a_lam_k1": a_lam_k1, "a_lam_q2": a_lam_q2,
        "a_lam_k2": a_lam_k2, "a_subln_w": a_subln_w, "a_w_out": a_w_out,
        "b_w_in": b_w_in, "b_forget_bias": b_forget_bias, "b_w_out": b_w_out,
        "ffn_mod_w": ffn_mod_w, "ffn_mod_b": ffn_mod_b, "ffn_ln_g": ffn_ln_g, "ffn_ln_b": ffn_ln_b,
        "moe_w_group": moe_w_group, "moe_b_group": moe_b_group,
        "moe_w_router": moe_w_router, "moe_b_router": moe_b_router,
        "moe_w_gate": moe_w_gate, "moe_w_up": moe_w_up, "moe_w_down": moe_w_down,
    }


def reference(x, c, mix_mod_w, mix_mod_b, mix_ln_g, mix_ln_b,
              a_w_in, a_lam_q1, a_lam_k1, a_lam_q2, a_lam_k2, a_subln_w, a_w_out,
              b_w_in, b_forget_bias, b_w_out,
              ffn_mod_w, ffn_mod_b, ffn_ln_g, ffn_ln_b,
              moe_w_group, moe_b_group, moe_w_router, moe_b_router,
              moe_w_gate, moe_w_up, moe_w_down):
    S = x.shape[1]
    positions = jnp.arange(S, dtype=jnp.int32)
    for i in range(DEPTH):
        j = i // N_MIXERS
        shift, scale, gate = adaln_modulation(c, mix_mod_w[i], mix_mod_b[i])
        h = x * (1.0 + scale) + shift
        if i % N_MIXERS == 0:
            y = differential_attention(h, positions, a_w_in[j], a_lam_q1[j], a_lam_k1[j],
                                       a_lam_q2[j], a_lam_k2[j], a_subln_w[j], a_w_out[j],
                                       diff_lambda_init(i + 1))
        else:
            y = forgetting_attention(h, b_w_in[j], b_forget_bias[j], b_w_out[j])
        x = layer_norm(DEEPNORM_ALPHA * x + gate * y, mix_ln_g[i], mix_ln_b[i])
        shift, scale, gate = adaln_modulation(c, ffn_mod_w[i], ffn_mod_b[i])
        h = x * (1.0 + scale) + shift
        y = hierarchical_moe(h, moe_w_group[i], moe_b_group[i], moe_w_router[i], moe_b_router[i],
                             moe_w_gate[i], moe_w_up[i], moe_w_down[i])
        x = layer_norm(DEEPNORM_ALPHA * x + gate * y, ffn_ln_g[i], ffn_ln_b[i])
    return x
```

```python
import functools
import math

import jax
import jax.numpy as jnp
from jax import lax
from jax.experimental import pallas as pl
from jax.experimental.pallas import tpu as pltpu

F32 = jnp.float32
BF16 = jnp.bfloat16
U32 = jnp.uint32
I32 = jnp.int32
HIGHEST = lax.Precision.HIGHEST

HEAD_DIM = 128
LANES = 128
SUBLANES = 8
ROPE_THETA = 10000.0
LN_EPS = 1e-5
RMS_EPS = 1e-5
TOP_K = 2
VMEM_LIMIT = 56 * 1024 * 1024

NT_DIMS = (((1,), (1,)), ((), ()))


def _params(*sem):
    return pltpu.CompilerParams(dimension_semantics=sem, vmem_limit_bytes=VMEM_LIMIT)


def _pick(total, want):
    t = min(total, want)
    assert total % t == 0, (total, want)
    return t


def _mod_kernel(c_ref, w_ref, b_ref, o_ref):
    c = c_ref[...]
    act = c * jax.nn.sigmoid(c)
    o_ref[0] = jnp.dot(act, w_ref[0], preferred_element_type=F32, precision=HIGHEST) + b_ref[0]


def _modulation(c_pad, w, b):
    L, D, D3 = w.shape
    tn = _pick(D3, 768)
    return pl.pallas_call(
        _mod_kernel,
        grid=(L, D3 // tn),
        in_specs=[pl.BlockSpec((SUBLANES, D), lambda l, j: (0, 0)),
                  pl.BlockSpec((1, D, tn), lambda l, j: (l, 0, j)),
                  pl.BlockSpec((1, 1, tn), lambda l, j: (l, 0, j))],
        out_specs=pl.BlockSpec((1, SUBLANES, tn), lambda l, j: (l, 0, j)),
        out_shape=jax.ShapeDtypeStruct((L, SUBLANES, D3), F32),
        compiler_params=_params("arbitrary", "arbitrary"),
        name="adaln_modulation",
    )(c_pad, w, b.reshape(L, 1, D3))


def _inproj_kernel(*refs, rope, has_gate, n_q_tiles, n_rope_tiles, q_scale, tn):
    it = iter(refs)
    x_ref, sc_ref, sh_ref, w_ref = next(it), next(it), next(it), next(it)
    cos_ref = sin_ref = wf_ref = f_ref = None
    if rope:
        cos_ref, sin_ref = next(it), next(it)
    if has_gate:
        wf_ref = next(it)
    o_ref = next(it)
    if has_gate:
        f_ref = next(it)
    h_ref = next(it)
    j = pl.program_id(1)

    @pl.when(j == 0)
    def _():
        h = x_ref[...] * (1.0 + sc_ref[0]) + sh_ref[0]
        h_ref[...] = h.astype(BF16)
        if has_gate:
            f_ref[...] = jnp.dot(h, wf_ref[...], preferred_element_type=F32, precision=HIGHEST)

    o = jnp.dot(h_ref[...], w_ref[...], preferred_element_type=F32)
    mult = jnp.where(j < n_q_tiles, jnp.float32(q_scale), jnp.float32(1.0))
    if rope:
        @pl.when(j < n_rope_tiles)
        def _():
            cos = cos_ref[...]
            sin = sin_ref[...]
            for k in range(tn // HEAD_DIM):
                blk = o[:, k * HEAD_DIM:(k + 1) * HEAD_DIM]
                rot = pltpu.roll(blk, HEAD_DIM // 2, 1)
                o_ref[:, k * HEAD_DIM:(k + 1) * HEAD_DIM] = ((blk * cos + rot * sin) * mult).astype(BF16)

        @pl.when(j >= n_rope_tiles)
        def _():
            o_ref[...] = o.astype(BF16)
    else:
        o_ref[...] = (o * mult).astype(BF16)


def _inproj(x2d, scale, shift, w_bf, *, seq, q_width, rope_tabs=None, rope_width=0, w_gate=None):
    N, D = x2d.shape
    W = w_bf.shape[1]
    tm = _pick(seq, 512)
    tn = _pick(q_width, 512)
    assert W % tn == 0 and rope_width % tn == 0
    tiles_per_seq = seq // tm
    rope = rope_tabs is not None
    has_gate = w_gate is not None
    in_specs = [pl.BlockSpec((tm, D), lambda i, j: (i, 0)),
                pl.BlockSpec((1, 1, D), lambda i, j: (i // tiles_per_seq, 0, 0)),
                pl.BlockSpec((1, 1, D), lambda i, j: (i // tiles_per_seq, 0, 0)),
                pl.BlockSpec((D, tn), lambda i, j: (0, j))]
    args = [x2d, scale, shift, w_bf]
    if rope:
        tab_spec = pl.BlockSpec((tm, HEAD_DIM), lambda i, j: (i % tiles_per_seq, 0))
        in_specs += [tab_spec, tab_spec]
        args += list(rope_tabs)
    out_specs = [pl.BlockSpec((tm, tn), lambda i, j: (i, j))]
    out_shape = [jax.ShapeDtypeStruct((N, W), BF16)]
    if has_gate:
        in_specs.append(pl.BlockSpec((D, LANES), lambda i, j: (0, 0)))
        args.append(w_gate)
        out_specs.append(pl.BlockSpec((tm, LANES), lambda i, j: (i, 0)))
        out_shape.append(jax.ShapeDtypeStruct((N, LANES), F32))
    kern = functools.partial(_inproj_kernel, rope=rope, has_gate=has_gate, n_q_tiles=q_width // tn,
                             n_rope_tiles=rope_width // tn, q_scale=HEAD_DIM ** -0.5, tn=tn)
    outs = pl.pallas_call(
        kern,
        grid=(N // tm, W // tn),
        in_specs=in_specs,
        out_specs=out_specs,
        out_shape=out_shape,
        scratch_shapes=[pltpu.VMEM((tm, D), BF16)],
        compiler_params=_params("arbitrary", "arbitrary"),
        name="modulate_inproj",
    )(*args)
    return tuple(outs) if has_gate else outs[0]


def _gate_cumsum_kernel(f_ref, b_ref, o_ref, carry_ref, *, tc):
    @pl.when(pl.program_id(1) == 0)
    def _():
        carry_ref[...] = jnp.zeros_like(carry_ref)

    z = f_ref[...] + b_ref[...]
    log_f = jnp.minimum(z, 0.0) - jnp.log1p(jnp.exp(-jnp.abs(z)))
    row = lax.broadcasted_iota(I32, (tc, tc), 0)
    col = lax.broadcasted_iota(I32, (tc, tc), 1)
    tri = jnp.where(row >= col, 1.0, 0.0).astype(F32)
    cum = jnp.dot(tri, log_f, preferred_element_type=F32, precision=HIGHEST) + carry_ref[0:1, :]
    o_ref[...] = cum
    carry_ref[...] = jnp.broadcast_to(cum[tc - 1:tc, :], carry_ref.shape)


def _gate_cumsum(f_logit, bias_pad, *, batch, seq):
    N = f_logit.shape[0]
    tc = _pick(seq, 512)
    per = seq // tc
    return pl.pallas_call(
        functools.partial(_gate_cumsum_kernel, tc=tc),
        grid=(batch, per),
        in_specs=[pl.BlockSpec((tc, LANES), lambda b, i: (b * per + i, 0)),
                  pl.BlockSpec((1, LANES), lambda b, i: (0, 0))],
        out_specs=pl.BlockSpec((tc, LANES), lambda b, i: (b * per + i, 0)),
        out_shape=jax.ShapeDtypeStruct((N, LANES), F32),
        scratch_shapes=[pltpu.VMEM((SUBLANES, LANES), F32)],
        compiler_params=_params("arbitrary", "arbitrary"),
        name="forget_gate_cumsum",
    )(f_logit, bias_pad)


def _flash_step(q, k, v, bias, mask, carry):
    m, l, acc = carry
    s = lax.dot_general(q, k, NT_DIMS, preferred_element_type=F32)
    if bias is not None:
        s = s + bias
    if mask is not None:
        s = jnp.where(mask, s, -jnp.inf)
    m_new = jnp.maximum(m, jnp.max(s, axis=1, keepdims=True))
    alpha = jnp.exp(m - m_new)
    p = jnp.exp(s - m_new)
    l = alpha * l + jnp.sum(p, axis=1, keepdims=True)
    acc = alpha * acc + jnp.dot(p.astype(BF16), v, preferred_element_type=F32)
    return m_new, l, acc


def _flash_init(tq, dv):
    return (jnp.full((tq, 1), -jnp.inf, F32), jnp.zeros((tq, 1), F32), jnp.zeros((tq, dv), F32))


def _causal_mask(t):
    return lax.broadcasted_iota(I32, (t, t), 0) >= lax.broadcasted_iota(I32, (t, t), 1)


def _fox_attn_kernel(q_ref, k_ref, v_ref, cum_ref, crow_ref, o_ref, ccol_ref, *, t, seq):
    h = pl.program_id(1)
    qi = pl.program_id(2)

    @pl.when(qi == 0)
    def _():
        lane = lax.broadcasted_iota(I32, (seq, LANES), 1)
        col = jnp.sum(jnp.where(lane == h, cum_ref[...], 0.0), axis=1, keepdims=True)
        ccol_ref[...] = jnp.broadcast_to(col, (seq, LANES))

    q = q_ref[...]
    q0 = pl.multiple_of(qi * t, t)
    cq = ccol_ref[pl.ds(q0, t), :][:, 0:1]

    def block(j, carry, mask):
        k0 = pl.multiple_of(j * t, t)
        bias = cq - crow_ref[0, :, pl.ds(k0, t)]
        return _flash_step(q, k_ref[pl.ds(k0, t), :], v_ref[pl.ds(k0, t), :], bias, mask, carry)

    carry = lax.fori_loop(0, qi, lambda j, c: block(j, c, None), _flash_init(t, HEAD_DIM))
    _, l, acc = block(qi, carry, _causal_mask(t))
    o_ref[...] = (acc / l).astype(BF16)


def _fox_attention(proj, cum, cum_rows, *, batch, seq, heads):
    N = proj.shape[0]
    t = _pick(seq, 512)
    nq = seq // t
    return pl.pallas_call(
        functools.partial(_fox_attn_kernel, t=t, seq=seq),
        grid=(batch, heads, nq),
        in_specs=[pl.BlockSpec((t, HEAD_DIM), lambda b, h, i: (b * nq + i, h)),
                  pl.BlockSpec((seq, HEAD_DIM), lambda b, h, i: (b, heads + h)),
                  pl.BlockSpec((seq, HEAD_DIM), lambda b, h, i: (b, 2 * heads + h)),
                  pl.BlockSpec((seq, LANES), lambda b, h, i: (b, 0)),
                  pl.BlockSpec((1, 1, seq), lambda b, h, i: (b * heads + h, 0, 0))],
        out_specs=pl.BlockSpec((t, HEAD_DIM), lambda b, h, i: (b * nq + i, h)),
        out_shape=jax.ShapeDtypeStruct((N, heads * HEAD_DIM), BF16),
        scratch_shapes=[pltpu.VMEM((seq, LANES), F32)],
        compiler_params=_params("arbitrary", "arbitrary", "arbitrary"),
        name="forgetting_attention",
    )(proj, proj, proj, cum, cum_rows)


def _diff_attn_kernel(q_ref, k_ref, v_ref, lam_ref, subln_ref, o_ref, *, t, lam_init):
    qi = pl.program_id(2)
    d = HEAD_DIM
    q1 = q_ref[:, 0:d]
    q2 = q_ref[:, d:2 * d]

    def block(j, carry, mask):
        k0 = pl.multiple_of(j * t, t)
        v = v_ref[pl.ds(k0, t), :]
        c1 = _flash_step(q1, k_ref[pl.ds(k0, t), 0:d], v, None, mask, carry[0])
        c2 = _flash_step(q2, k_ref[pl.ds(k0, t), d:2 * d], v, None, mask, carry[1])
        return c1, c2

    init = (_flash_init(t, 2 * d), _flash_init(t, 2 * d))
    carry = lax.fori_loop(0, qi, lambda j, c: block(j, c, None), init)
    (_, l1, a1), (_, l2, a2) = block(qi, carry, _causal_mask(t))

    lv = lam_ref[...]
    lam = (jnp.exp(jnp.sum(lv[0:1] * lv[1:2], axis=1, keepdims=True))
           - jnp.exp(jnp.sum(lv[2:3] * lv[3:4], axis=1, keepdims=True)) + lam_init)
    o = a1 / l1 - lam * (a2 / l2)
    o = o * lax.rsqrt(jnp.mean(o * o, axis=1, keepdims=True) + RMS_EPS)
    o_ref[...] = (o * subln_ref[...] * (1.0 - lam_init)).astype(BF16)


def _diff_attention(proj, lam_vecs, subln_w, *, batch, seq, heads, lam_init):
    N = proj.shape[0]
    t = _pick(seq, 512)
    nq = seq // t
    dv = 2 * HEAD_DIM
    return pl.pallas_call(
        functools.partial(_diff_attn_kernel, t=t, lam_init=lam_init),
        grid=(batch, heads, nq),
        in_specs=[pl.BlockSpec((t, dv), lambda b, h, i: (b * nq + i, h)),
                  pl.BlockSpec((seq, dv), lambda b, h, i: (b, heads + h)),
                  pl.BlockSpec((seq, dv), lambda b, h, i: (b, 2 * heads + h)),
                  pl.BlockSpec((4, HEAD_DIM), lambda b, h, i: (0, 0)),
                  pl.BlockSpec((1, dv), lambda b, h, i: (0, 0))],
        out_specs=pl.BlockSpec((t, dv), lambda b, h, i: (b * nq + i, h)),
        out_shape=jax.ShapeDtypeStruct((N, heads * dv), BF16),
        compiler_params=_params("arbitrary", "arbitrary", "arbitrary"),
        name="differential_attention",
    )(proj, proj, proj, lam_vecs, subln_w)


def _pack_halves(y):
    w = y.shape[1] // 2
    lo = lax.bitcast_convert_type(y[:, :w].astype(BF16).astype(F32), U32)
    hi = lax.bitcast_convert_type(y[:, w:].astype(BF16).astype(F32), U32)
    return (lo >> 16) | (hi & jnp.uint32(0xFFFF0000))


def _unpack_halves(u):
    lo = lax.bitcast_convert_type(u << 16, F32)
    hi = lax.bitcast_convert_type(u & jnp.uint32(0xFFFF0000), F32)
    return lo, hi


def _layer_norm(z, g, b):
    mu = jnp.mean(z, axis=1, keepdims=True)
    zc = z - mu
    var = jnp.mean(zc * zc, axis=1, keepdims=True)
    return zc * lax.rsqrt(var + LN_EPS) * g + b


def _outproj_ln_kernel(o_ref, w_ref, x_ref, gate_ref, g_ref, b_ref, sc_ref, sh_ref, xo_ref, hp_ref, *, alpha):
    y = jnp.dot(o_ref[...], w_ref[...], preferred_element_type=F32)
    z = alpha * x_ref[...] + (1.0 + gate_ref[0]) * y
    xn = _layer_norm(z, g_ref[...], b_ref[...])
    xo_ref[...] = xn
    hp_ref[...] = _pack_halves(xn * (1.0 + sc_ref[0]) + sh_ref[0])


def _outproj_ln(o, w_bf, x2d, gate, ln_g, ln_b, scale2, shift2, *, seq, alpha):
    N, D = x2d.shape
    K = o.shape[1]
    tm = _pick(seq, 256)
    per = seq // tm
    bspec = pl.BlockSpec((1, 1, D), lambda i: (i // per, 0, 0))
    vspec = pl.BlockSpec((1, D), lambda i: (0, 0))
    return pl.pallas_call(
        functools.partial(_outproj_ln_kernel, alpha=alpha),
        grid=(N // tm,),
        in_specs=[pl.BlockSpec((tm, K), lambda i: (i, 0)),
                  pl.BlockSpec((K, D), lambda i: (0, 0)),
                  pl.BlockSpec((tm, D), lambda i: (i, 0)),
                  bspec, vspec, vspec, bspec, bspec],
        out_specs=[pl.BlockSpec((tm, D), lambda i: (i, 0)),
                   pl.BlockSpec((tm, D // 2), lambda i: (i, 0))],
        out_shape=[jax.ShapeDtypeStruct((N, D), F32), jax.ShapeDtypeStruct((N, D // 2), U32)],
        compiler_params=_params("arbitrary"),
        name="outproj_layernorm",
    )(o, w_bf, x2d, gate, ln_g, ln_b, scale2, shift2)


def _router_kernel(x_ref, sc_ref, sh_ref, w_ref, b_ref, ids_ref, wcol_ref, cnt_ref, carry_ref,
                   *, tm, groups, experts):
    @pl.when(pl.program_id(0) == 0)
    def _():
        carry_ref[...] = jnp.zeros_like(carry_ref)

    n_exp = groups * experts
    h = x_ref[...] * (1.0 + sc_ref[0]) + sh_ref[0]
    logits = lax.dot_general(w_ref[...], h, NT_DIMS, preferred_element_type=F32, precision=HIGHEST)
    logits = logits + b_ref[:, 0:1]

    gl = logits[0:groups]
    gmax = jnp.max(gl, axis=0, keepdims=True)
    gid = lax.broadcasted_iota(I32, gl.shape, 0).astype(F32)
    gidx = jnp.min(jnp.where(gl == gmax, gid, float(groups)), axis=0, keepdims=True)
    g_w = 1.0 / jnp.sum(jnp.exp(gl - gmax), axis=0, keepdims=True)

    el = logits[SUBLANES:SUBLANES + experts]
    for g in range(1, groups):
        el = jnp.where(gidx == float(g), logits[SUBLANES + g * experts:SUBLANES + (g + 1) * experts], el)
    eid = lax.broadcasted_iota(I32, el.shape, 0).astype(F32)
    v1 = jnp.max(el, axis=0, keepdims=True)
    i1 = jnp.min(jnp.where(el == v1, eid, float(experts)), axis=0, keepdims=True)
    el2 = jnp.where(eid == i1, -jnp.inf, el)
    v2 = jnp.max(el2, axis=0, keepdims=True)
    i2 = jnp.min(jnp.where(el2 == v2, eid, float(experts)), axis=0, keepdims=True)
    t = jnp.exp(v2 - v1)
    p1 = 1.0 / (1.0 + t)
    w1 = g_w * p1
    w2 = g_w * (t * p1)
    e1 = gidx * float(experts) + i1
    e2 = gidx * float(experts) + i2

    xid = lax.broadcasted_iota(I32, (n_exp, tm), 0).astype(F32)
    oh1 = xid == e1
    oh2 = xid == e2
    cnt = jnp.where(oh1, 1.0, 0.0) + jnp.where(oh2, 1.0, 0.0)
    before = (lax.broadcasted_iota(I32, (tm, tm), 0) < lax.broadcasted_iota(I32, (tm, tm), 1))
    prefix = jnp.dot(cnt.astype(BF16), jnp.where(before, 1.0, 0.0).astype(BF16), preferred_element_type=F32)
    base = carry_ref[:, 0:1] + prefix
    r1 = jnp.sum(jnp.where(oh1, base, 0.0), axis=0, keepdims=True)
    r2 = jnp.sum(jnp.where(oh2, base, 0.0), axis=0, keepdims=True)
    carry_ref[...] = carry_ref[...] + jnp.sum(cnt, axis=1, keepdims=True)
    cnt_ref[...] = carry_ref[...]

    ids_ref[...] = jnp.zeros_like(ids_ref)
    ids_ref[0:1, :] = e1.astype(I32)
    ids_ref[1:2, :] = e2.astype(I32)
    ids_ref[2:3, :] = r1.astype(I32)
    ids_ref[3:4, :] = r2.astype(I32)
    rid = lax.broadcasted_iota(I32, (LANES, tm), 0)
    wrows = jnp.where(rid == 0, w1, jnp.where(rid == 1, w2, 0.0))
    wcol_ref[...] = wrows.T


def _router(xn, scale, shift, w_t, b_col, *, seq, groups, experts):
    N, D = xn.shape
    tm = _pick(seq, 512)
    per = seq // tm
    rows = w_t.shape[0]
    n_exp = groups * experts
    bspec = pl.BlockSpec((1, 1, D), lambda i: (i // per, 0, 0))
    return pl.pallas_call(
        functools.partial(_router_kernel, tm=tm, groups=groups, experts=experts),
        grid=(N // tm,),
        in_specs=[pl.BlockSpec((tm, D), lambda i: (i, 0)), bspec, bspec,
                  pl.BlockSpec((rows, D), lambda i: (0, 0)),
                  pl.BlockSpec((rows, LANES), lambda i: (0, 0))],
        out_specs=[pl.BlockSpec((SUBLANES, tm), lambda i: (0, i)),
                   pl.BlockSpec((tm, LANES), lambda i: (i, 0)),
                   pl.BlockSpec((n_exp, LANES), lambda i: (0, 0))],
        out_shape=[jax.ShapeDtypeStruct((SUBLANES, N), I32),
                   jax.ShapeDtypeStruct((N, LANES), F32),
                   jax.ShapeDtypeStruct((n_exp, LANES), F32)],
        scratch_shapes=[pltpu.VMEM((n_exp, LANES), F32)],
        compiler_params=_params("arbitrary"),
        name="moe_router",
    )(xn, scale, shift, w_t, b_col)


def _dispatch_kernel(pos_ref, hp_ref, xs_in_ref, xs_ref, sem, *, tm, n_tok):
    del xs_in_ref
    base = pl.program_id(0) * tm

    def row_copy(r, p):
        return pltpu.make_async_copy(hp_ref.at[pl.ds(r, 1), :], xs_ref.at[pl.ds(p, 1), :], sem)

    def issue(r, c):
        row_copy(r, pos_ref[base + r]).start()
        row_copy(r, pos_ref[n_tok + base + r]).start()
        return c

    def drain(r, c):
        row_copy(0, 0).wait()
        row_copy(0, 0).wait()
        return c

    lax.fori_loop(0, tm, issue, 0)
    lax.fori_loop(0, tm, drain, 0)


def _dispatch(pos, hp, n_slots):
    N, D2 = hp.shape
    tm = _pick(N, 256)
    grid_spec = pltpu.PrefetchScalarGridSpec(
        num_scalar_prefetch=1,
        grid=(N // tm,),
        in_specs=[pl.BlockSpec((tm, D2), lambda i, pos: (i, 0)),
                  pl.BlockSpec(memory_space=pl.ANY)],
        out_specs=pl.BlockSpec(memory_space=pl.ANY),
        scratch_shapes=[pltpu.SemaphoreType.DMA(())],
    )
    return pl.pallas_call(
        functools.partial(_dispatch_kernel, tm=tm, n_tok=N),
        grid_spec=grid_spec,
        out_shape=jax.ShapeDtypeStruct((n_slots, D2), U32),
        input_output_aliases={2: 0},
        compiler_params=_params("arbitrary"),
        name="moe_dispatch",
    )(pos, hp, jnp.zeros((n_slots, D2), U32))


def _expert_kernel(te_ref, tb_ref, tv_ref, xs_ref, wg_ref, wu_ref, wd_ref, ys_ref, wg_bf, wu_bf, wd_bf):
    j = pl.program_id(0)
    prev = te_ref[jnp.maximum(j - 1, 0)]

    @pl.when(jnp.logical_or(j == 0, te_ref[j] != prev))
    def _():
        wg_bf[...] = wg_ref[0].astype(BF16)
        wu_bf[...] = wu_ref[0].astype(BF16)
        wd_bf[...] = wd_ref[0].astype(BF16)

    @pl.when(tv_ref[j] == 1)
    def _():
        lo, hi = _unpack_halves(xs_ref[...])
        x = jnp.concatenate([lo.astype(BF16), hi.astype(BF16)], axis=1)
        a = jnp.dot(x, wg_bf[...], preferred_element_type=F32)
        u = jnp.dot(x, wu_bf[...], preferred_element_type=F32)
        hid = (a * jax.nn.sigmoid(a) * u).astype(BF16)
        ys_ref[...] = _pack_halves(jnp.dot(hid, wd_bf[...], preferred_element_type=F32))


def _experts(tile_expert, tile_block, tile_valid, xs, w_gate, w_up, w_down, *, tile):
    P, D2 = xs.shape
    _, D, F = w_gate.shape
    n_tiles = P // tile
    grid_spec = pltpu.PrefetchScalarGridSpec(
        num_scalar_prefetch=3,
        grid=(n_tiles,),
        in_specs=[pl.BlockSpec((tile, D2), lambda j, te, tb, tv: (tb[j], 0)),
                  pl.BlockSpec((1, D, F), lambda j, te, tb, tv: (te[j], 0, 0)),
                  pl.BlockSpec((1, D, F), lambda j, te, tb, tv: (te[j], 0, 0)),
                  pl.BlockSpec((1, F, D), lambda j, te, tb, tv: (te[j], 0, 0))],
        out_specs=pl.BlockSpec((tile, D2), lambda j, te, tb, tv: (tb[j], 0)),
        scratch_shapes=[pltpu.VMEM((D, F), BF16), pltpu.VMEM((D, F), BF16), pltpu.VMEM((F, D), BF16)],
    )
    return pl.pallas_call(
        _expert_kernel,
        grid_spec=grid_spec,
        out_shape=jax.ShapeDtypeStruct((P, D2), U32),
        input_output_aliases={3: 0},
        compiler_params=_params("arbitrary"),
        name="moe_experts",
    )(tile_expert, tile_block, tile_valid, xs, w_gate, w_up, w_down)


def _combine_ln_kernel(pos_ref, x_ref, wcol_ref, gate_ref, g_ref, b_ref, ys_ref, o_ref, buf, sem,
                       *, tm, n_tok, alpha):
    base = pl.program_id(0) * tm

    def row_copy(k, r, p):
        return pltpu.make_async_copy(ys_ref.at[pl.ds(p, 1), :], buf.at[k, pl.ds(r, 1), :], sem)

    def issue(r, c):
        row_copy(0, r, pos_ref[base + r]).start()
        row_copy(1, r, pos_ref[n_tok + base + r]).start()
        return c

    def drain(r, c):
        row_copy(0, 0, 0).wait()
        row_copy(1, 0, 0).wait()
        return c

    lax.fori_loop(0, tm, issue, 0)
    lax.fori_loop(0, tm, drain, 0)

    w = wcol_ref[...]
    lo1, hi1 = _unpack_halves(buf[0])
    lo2, hi2 = _unpack_halves(buf[1])
    w1 = w[:, 0:1]
    w2 = w[:, 1:2]
    y = jnp.concatenate([w1 * lo1 + w2 * lo2, w1 * hi1 + w2 * hi2], axis=1)
    z = alpha * x_ref[...] + (1.0 + gate_ref[0]) * y
    o_ref[...] = _layer_norm(z, g_ref[...], b_ref[...])


def _combine_ln(pos, xn, wcol, gate, ln_g, ln_b, ys, *, seq, alpha):
    N, D = xn.shape
    D2 = ys.shape[1]
    tm = _pick(seq, 256)
    per = seq // tm
    grid_spec = pltpu.PrefetchScalarGridSpec(
        num_scalar_prefetch=1,
        grid=(N // tm,),
        in_specs=[pl.BlockSpec((tm, D), lambda i, pos: (i, 0)),
                  pl.BlockSpec((tm, LANES), lambda i, pos: (i, 0)),
                  pl.BlockSpec((1, 1, D), lambda i, pos: (i // per, 0, 0)),
                  pl.BlockSpec((1, D), lambda i, pos: (0, 0)),
                  pl.BlockSpec((1, D), lambda i, pos: (0, 0)),
                  pl.BlockSpec(memory_space=pl.ANY)],
        out_specs=pl.BlockSpec((tm, D), lambda i, pos: (i, 0)),
        scratch_shapes=[pltpu.VMEM((TOP_K, tm, D2), U32), pltpu.SemaphoreType.DMA(())],
    )
    return pl.pallas_call(
        functools.partial(_combine_ln_kernel, tm=tm, n_tok=N, alpha=alpha),
        grid_spec=grid_spec,
        out_shape=jax.ShapeDtypeStruct((N, D), F32),
        compiler_params=_params("arbitrary"),
        name="moe_combine_layernorm",
    )(pos, xn, wcol, gate, ln_g, ln_b, ys)


def _moe_sublayer(xn, hp, scale, shift, gate, ln_g, ln_b, w_group, b_group, w_router, b_router,
                  w_gate, w_up, w_down, layer, *, seq, alpha, tile):
    N, D = xn.shape
    G, _, E = w_router.shape
    n_exp = G * E
    w_t = jnp.concatenate([w_group.T, jnp.zeros((SUBLANES - G, D), F32),
                           w_router.transpose(0, 2, 1).reshape(n_exp, D)], axis=0)
    b_col = jnp.concatenate([b_group, jnp.zeros((SUBLANES - G,), F32), b_router.reshape(n_exp)])
    b_col = jnp.broadcast_to(b_col[:, None], (SUBLANES + n_exp, LANES))
    ids, wcol, counts = _router(xn, scale, shift, w_t, b_col, seq=seq, groups=G, experts=E)

    cnt = counts[:, 0].astype(I32)
    tiles_per = (cnt + tile - 1) // tile
    tile_end = jnp.cumsum(tiles_per)
    offs = (tile_end - tiles_per) * tile
    pos = jnp.concatenate([offs[ids[0]] + ids[2], offs[ids[1]] + ids[3]])
    n_tiles = (TOP_K * N) // tile + n_exp
    n_used = tile_end[n_exp - 1]
    jt = jnp.arange(n_tiles, dtype=I32)
    last = jnp.maximum(n_used - 1, 0)
    tile_block = jnp.minimum(jt, last)
    tile_expert = jnp.searchsorted(tile_end, tile_block, side="right").astype(I32)
    tile_valid = (jt < n_used).astype(I32)

    xs = _dispatch(pos, hp, n_tiles * tile)
    ys = _experts(tile_expert + layer * n_exp, tile_block, tile_valid, xs, w_gate, w_up, w_down, tile=tile)
    return _combine_ln(pos, xn, wcol, gate, ln_g, ln_b, ys, seq=seq, alpha=alpha)


def _rope_tables(seq):
    inv_freq = jnp.power(ROPE_THETA, -jnp.arange(0, HEAD_DIM, 2, dtype=F32) / HEAD_DIM)
    ang = jnp.arange(seq, dtype=F32)[:, None] * inv_freq[None, :]
    cos, sin = jnp.cos(ang), jnp.sin(ang)
    return jnp.concatenate([cos, cos], axis=1), jnp.concatenate([-sin, sin], axis=1)


def kernel(x, c, mix_mod_w, mix_mod_b, mix_ln_g, mix_ln_b, a_w_in, a_lam_q1, a_lam_k1, a_lam_q2, a_lam_k2, a_subln_w, a_w_out, b_w_in, b_forget_bias, b_w_out, ffn_mod_w, ffn_mod_b, ffn_ln_g, ffn_ln_b, moe_w_group, moe_b_group, moe_w_router, moe_b_router, moe_w_gate, moe_w_up, moe_w_down):
    B, S, D = x.shape
    depth = mix_mod_w.shape[0]
    N = B * S
    alpha = (2.0 * depth) ** 0.25
    da_heads = D // (2 * HEAD_DIM)
    fx_heads = D // HEAD_DIM
    G, E, _, F = moe_w_gate.shape[1:]
    assert B <= SUBLANES and fx_heads <= LANES and G <= SUBLANES

    c_pad = jnp.zeros((SUBLANES, D), F32).at[:B].set(c)
    mix_mod = _modulation(c_pad, mix_mod_w, mix_mod_b)[:, :B]
    ffn_mod = _modulation(c_pad, ffn_mod_w, ffn_mod_b)[:, :B]

    def split(m):
        return tuple(m[:, None, k * D:(k + 1) * D] for k in range(3))

    rope_tabs = _rope_tables(S)
    w_gate_all = moe_w_gate.reshape(depth * G * E, D, F)
    w_up_all = moe_w_up.reshape(depth * G * E, D, F)
    w_down_all = moe_w_down.reshape(depth * G * E, F, D)

    x2d = x.reshape(N, D)
    for i in range(depth):
        j = i // 2
        shift, scale, gate = split(mix_mod[i])
        shift2, scale2, gate2 = split(ffn_mod[i])
        if i % 2 == 0:
            qk_width = 2 * da_heads * HEAD_DIM
            proj = _inproj(x2d, scale, shift, a_w_in[j].astype(BF16), seq=S, q_width=qk_width,
                           rope_tabs=rope_tabs, rope_width=2 * qk_width)
            lam_vecs = jnp.stack([a_lam_q1[j], a_lam_k1[j], a_lam_q2[j], a_lam_k2[j]])
            lam_init = 0.8 - 0.6 * math.exp(-0.3 * i)
            o = _diff_attention(proj, lam_vecs, a_subln_w[j][None, :], batch=B, seq=S, heads=da_heads,
                                lam_init=lam_init)
            w_out = a_w_out[j]
        else:
            width = fx_heads * HEAD_DIM
            w_in = b_w_in[j]
            w_f = jnp.zeros((D, LANES), F32).at[:, :fx_heads].set(w_in[:, 3 * width:])
            proj, f_logit = _inproj(x2d, scale, shift, w_in[:, :3 * width].astype(BF16), seq=S,
                                    q_width=width, w_gate=w_f)
            bias = jnp.zeros((1, LANES), F32).at[0, :fx_heads].set(b_forget_bias[j])
            cum = _gate_cumsum(f_logit, bias, batch=B, seq=S)
            cum_rows = cum[:, :fx_heads].reshape(B, S, fx_heads).transpose(0, 2, 1).reshape(B * fx_heads, 1, S)
            o = _fox_attention(proj, cum, cum_rows, batch=B, seq=S, heads=fx_heads)
            w_out = b_w_out[j]
        xn, hp = _outproj_ln(o, w_out.astype(BF16), x2d, gate, mix_ln_g[i][None, :], mix_ln_b[i][None, :],
                             scale2, shift2, seq=S, alpha=alpha)
        x2d = _moe_sublayer(xn, hp, scale2, shift2, gate2, ffn_ln_g[i][None, :], ffn_ln_b[i][None, :],
                            moe_w_group[i], moe_b_group[i], moe_w_router[i], moe_b_router[i],
                            w_gate_all, w_up_all, w_down_all, i, seq=S, alpha=alpha, tile=256)
    return x2d.reshape(B, S, D)
```

```python
import functools
import math

import jax
import jax.numpy as jnp
from jax import lax
from jax.experimental import pallas as pl
from jax.experimental.pallas import tpu as pltpu

F32 = jnp.float32
BF16 = jnp.bfloat16
U32 = jnp.uint32
I32 = jnp.int32
HIGHEST = lax.Precision.HIGHEST

HEAD_DIM = 128
LANES = 128
SUBLANES = 8
ROPE_THETA = 10000.0
LN_EPS = 1e-5
RMS_EPS = 1e-5
TOP_K = 2
VMEM_LIMIT = 56 * 1024 * 1024

NT_DIMS = (((1,), (1,)), ((), ()))
LOG2E = math.log2(math.e)


def _params(*sem):
    return pltpu.CompilerParams(dimension_semantics=sem, vmem_limit_bytes=VMEM_LIMIT)


def _pick(total, want):
    t = min(total, want)
    assert total % t == 0, (total, want)
    return t


def _mod_kernel(c_ref, w_ref, b_ref, o_ref):
    c = c_ref[...]
    act = c * jax.nn.sigmoid(c)
    o_ref[0] = jnp.dot(act, w_ref[0], preferred_element_type=F32, precision=HIGHEST) + b_ref[0]


def _modulation(c_pad, w, b):
    L, D, D3 = w.shape
    tn = _pick(D3, 768)
    return pl.pallas_call(
        _mod_kernel,
        grid=(L, D3 // tn),
        in_specs=[pl.BlockSpec((SUBLANES, D), lambda l, j: (0, 0)),
                  pl.BlockSpec((1, D, tn), lambda l, j: (l, 0, j)),
                  pl.BlockSpec((1, 1, tn), lambda l, j: (l, 0, j))],
        out_specs=pl.BlockSpec((1, SUBLANES, tn), lambda l, j: (l, 0, j)),
        out_shape=jax.ShapeDtypeStruct((L, SUBLANES, D3), F32),
        compiler_params=_params("arbitrary", "arbitrary"),
        name="adaln_modulation",
    )(c_pad, w, b.reshape(L, 1, D3))


def _inproj_kernel(*refs, rope, has_gate, n_q_tiles, n_rope_tiles, q_scale, tn):
    it = iter(refs)
    x_ref, sc_ref, sh_ref, w_ref = next(it), next(it), next(it), next(it)
    cos_ref = sin_ref = wf_ref = f_ref = None
    if rope:
        cos_ref, sin_ref = next(it), next(it)
    if has_gate:
        wf_ref = next(it)
    o_ref = next(it)
    if has_gate:
        f_ref = next(it)
    h_ref = next(it)
    j = pl.program_id(1)

    @pl.when(j == 0)
    def _():
        h = x_ref[...] * (1.0 + sc_ref[0]) + sh_ref[0]
        h_ref[...] = h.astype(BF16)
        if has_gate:
            f_ref[...] = jnp.dot(h, wf_ref[...], preferred_element_type=F32, precision=HIGHEST)

    o = jnp.dot(h_ref[...], w_ref[...], preferred_element_type=F32)
    mult = jnp.where(j < n_q_tiles, jnp.float32(q_scale), jnp.float32(1.0))
    if rope:
        @pl.when(j < n_rope_tiles)
        def _():
            cos = cos_ref[...]
            sin = sin_ref[...]
            for k in range(tn // HEAD_DIM):
                blk = o[:, k * HEAD_DIM:(k + 1) * HEAD_DIM]
                rot = pltpu.roll(blk, HEAD_DIM // 2, 1)
                o_ref[:, k * HEAD_DIM:(k + 1) * HEAD_DIM] = ((blk * cos + rot * sin) * mult).astype(BF16)

        @pl.when(j >= n_rope_tiles)
        def _():
            o_ref[...] = o.astype(BF16)
    else:
        o_ref[...] = (o * mult).astype(BF16)


def _inproj(x2d, scale, shift, w_bf, *, seq, q_width, rope_tabs=None, rope_width=0, w_gate=None):
    N, D = x2d.shape
    W = w_bf.shape[1]
    tm = _pick(seq, 512)
    tn = _pick(q_width, 512)
    assert W % tn == 0 and rope_width % tn == 0
    tiles_per_seq = seq // tm
    rope = rope_tabs is not None
    has_gate = w_gate is not None
    in_specs = [pl.BlockSpec((tm, D), lambda i, j: (i, 0)),
                pl.BlockSpec((1, 1, D), lambda i, j: (i // tiles_per_seq, 0, 0)),
                pl.BlockSpec((1, 1, D), lambda i, j: (i // tiles_per_seq, 0, 0)),
                pl.BlockSpec((D, tn), lambda i, j: (0, j))]
    args = [x2d, scale, shift, w_bf]
    if rope:
        tab_spec = pl.BlockSpec((tm, HEAD_DIM), lambda i, j: (i % tiles_per_seq, 0))
        in_specs += [tab_spec, tab_spec]
        args += list(rope_tabs)
    out_specs = [pl.BlockSpec((tm, tn), lambda i, j: (i, j))]
    out_shape = [jax.ShapeDtypeStruct((N, W), BF16)]
    if has_gate:
        in_specs.append(pl.BlockSpec((D, LANES), lambda i, j: (0, 0)))
        args.append(w_gate)
        out_specs.append(pl.BlockSpec((tm, LANES), lambda i, j: (i, 0)))
        out_shape.append(jax.ShapeDtypeStruct((N, LANES), F32))
    kern = functools.partial(_inproj_kernel, rope=rope, has_gate=has_gate, n_q_tiles=q_width // tn,
                             n_rope_tiles=rope_width // tn, q_scale=HEAD_DIM ** -0.5 * LOG2E, tn=tn)
    outs = pl.pallas_call(
        kern,
        grid=(N // tm, W // tn),
        in_specs=in_specs,
        out_specs=out_specs,
        out_shape=out_shape,
        scratch_shapes=[pltpu.VMEM((tm, D), BF16)],
        compiler_params=_params("arbitrary", "arbitrary"),
        name="modulate_inproj",
    )(*args)
    return tuple(outs) if has_gate else outs[0]


def _gate_cumsum_kernel(f_ref, b_ref, o_ref, carry_ref, *, tc):
    @pl.when(pl.program_id(1) == 0)
    def _():
        carry_ref[...] = jnp.zeros_like(carry_ref)

    z = f_ref[...] + b_ref[...]
    log_f = jnp.minimum(z, 0.0) - jnp.log1p(jnp.exp(-jnp.abs(z)))
    row = lax.broadcasted_iota(I32, (tc, tc), 0)
    col = lax.broadcasted_iota(I32, (tc, tc), 1)
    tri = jnp.where(row >= col, 1.0, 0.0).astype(F32)
    cum = jnp.dot(tri, log_f, preferred_element_type=F32, precision=HIGHEST) + carry_ref[0:1, :]
    o_ref[...] = cum
    carry_ref[...] = jnp.broadcast_to(cum[tc - 1:tc, :], carry_ref.shape)


def _gate_cumsum(f_logit, bias_pad, *, batch, seq):
    N = f_logit.shape[0]
    tc = _pick(seq, 512)
    per = seq // tc
    return pl.pallas_call(
        functools.partial(_gate_cumsum_kernel, tc=tc),
        grid=(batch, per),
        in_specs=[pl.BlockSpec((tc, LANES), lambda b, i: (b * per + i, 0)),
                  pl.BlockSpec((1, LANES), lambda b, i: (0, 0))],
        out_specs=pl.BlockSpec((tc, LANES), lambda b, i: (b * per + i, 0)),
        out_shape=jax.ShapeDtypeStruct((N, LANES), F32),
        scratch_shapes=[pltpu.VMEM((SUBLANES, LANES), F32)],
        compiler_params=_params("arbitrary", "arbitrary"),
        name="forget_gate_cumsum",
    )(f_logit, bias_pad)


def _flash_update(streams, stats, dv, mask):
    m_ref, l_ref, acc_ref = stats
    scores = [lax.dot_general(k, q, NT_DIMS, preferred_element_type=F32) for q, k, _ in streams]
    probs, alphas = [], []
    for a, s in enumerate(scores):
        r = slice(SUBLANES * a, SUBLANES * a + 1)
        if mask is not None:
            s = jnp.where(mask, s, -jnp.inf)
        m_old = m_ref[r, :]
        m_new = jnp.maximum(m_old, jnp.max(s, axis=0, keepdims=True))
        alpha = jnp.exp2(m_old - m_new)
        p = jnp.exp2(s - m_new)
        l_ref[r, :] = alpha * l_ref[r, :] + jnp.sum(p, axis=0, keepdims=True)
        m_ref[r, :] = m_new
        probs.append(p.astype(BF16))
        alphas.append(alpha)
    for a, (_, _, v_t) in enumerate(streams):
        rows = slice(a * dv, (a + 1) * dv)
        acc_ref[rows, :] = alphas[a] * acc_ref[rows, :] + jnp.dot(v_t, probs[a], preferred_element_type=F32)


def _flash_reset(stats):
    m_ref, l_ref, acc_ref = stats
    m_ref[...] = jnp.full(m_ref.shape, -jnp.inf, F32)
    l_ref[...] = jnp.zeros(l_ref.shape, F32)
    acc_ref[...] = jnp.zeros(acc_ref.shape, F32)


def _flash_result(stats, a, dv):
    _, l_ref, acc_ref = stats
    return acc_ref[a * dv:(a + 1) * dv, :] * (1.0 / l_ref[SUBLANES * a:SUBLANES * a + 1, :])


def _causal_mask_t(t):
    return lax.broadcasted_iota(I32, (t, t), 1) >= lax.broadcasted_iota(I32, (t, t), 0)


def _store_transposed_blocks(v_ref, vt_ref, t):
    for jb in range(vt_ref.shape[0]):
        vt_ref[jb] = v_ref[jb * t:(jb + 1) * t, :].astype(F32).T.astype(BF16)


FOX_PACK = 4
N_SPLIT = 3


def _fox_attn_kernel(q_ref, k_ref, v_ref, cum_ref, o_ref, qx_ref, kx_ref, vt_ref, m_ref, l_ref, acc_ref,
                     *, t, seq):
    hp = pl.program_id(1)
    qi = pl.program_id(2)
    d = HEAD_DIM
    stats = (m_ref, l_ref, acc_ref)

    @pl.when(qi == 0)
    def _():
        _store_transposed_blocks(v_ref, vt_ref, t)
        lane = lax.broadcasted_iota(I32, (seq, LANES), 1)
        for a in range(FOX_PACK):
            c = jnp.sum(jnp.where(lane == hp * FOX_PACK + a, cum_ref[...], 0.0), axis=1, keepdims=True)
            rest = jnp.broadcast_to(c * LOG2E, (seq, LANES))
            qx = jnp.where(lane < 2 * N_SPLIT, 1.0, 0.0)
            kx = qx
            for i in range(N_SPLIT):
                piece = rest.astype(BF16).astype(F32)
                rest = rest - piece
                qx = jnp.where(lane == i, piece, qx)
                kx = jnp.where(lane == N_SPLIT + i, -piece, kx)
            qx_ref[a] = qx.astype(BF16)
            kx_ref[a] = kx.astype(BF16)

    _flash_reset(stats)
    q0 = pl.multiple_of(qi * t, t)
    qs = [jnp.concatenate([q_ref[:, a * d:(a + 1) * d], qx_ref[a, pl.ds(q0, t), :]], axis=1)
          for a in range(FOX_PACK)]

    def block(j, mask):
        k0 = pl.multiple_of(j * t, t)
        streams = []
        for a in range(FOX_PACK):
            k = jnp.concatenate([k_ref[pl.ds(k0, t), a * d:(a + 1) * d], kx_ref[a, pl.ds(k0, t), :]], axis=1)
            streams.append((qs[a], k, vt_ref[j, a * d:(a + 1) * d, :]))
        _flash_update(streams, stats, d, mask)

    def body(j, c):
        block(j, None)
        return c

    lax.fori_loop(0, qi, body, 0)
    block(qi, _causal_mask_t(t))
    for a in range(FOX_PACK):
        o_ref[:, a * d:(a + 1) * d] = _flash_result(stats, a, d).T.astype(BF16)


def _fox_attention(proj, cum, *, batch, seq, heads):
    N = proj.shape[0]
    t = _pick(seq, 512)
    nq = seq // t
    w = FOX_PACK * HEAD_DIM
    assert heads % FOX_PACK == 0
    hb = heads // FOX_PACK
    return pl.pallas_call(
        functools.partial(_fox_attn_kernel, t=t, seq=seq),
        grid=(batch, hb, nq),
        in_specs=[pl.BlockSpec((t, w), lambda b, h, i: (b * nq + i, h)),
                  pl.BlockSpec((seq, w), lambda b, h, i: (b, hb + h)),
                  pl.BlockSpec((seq, w), lambda b, h, i: (b, 2 * hb + h)),
                  pl.BlockSpec((seq, LANES), lambda b, h, i: (b, 0))],
        out_specs=pl.BlockSpec((t, w), lambda b, h, i: (b * nq + i, h)),
        out_shape=jax.ShapeDtypeStruct((N, heads * HEAD_DIM), BF16),
        scratch_shapes=[pltpu.VMEM((FOX_PACK, seq, LANES), BF16),
                        pltpu.VMEM((FOX_PACK, seq, LANES), BF16),
                        pltpu.VMEM((seq // t, w, t), BF16),
                        pltpu.VMEM((FOX_PACK * SUBLANES, t), F32),
                        pltpu.VMEM((FOX_PACK * SUBLANES, t), F32),
                        pltpu.VMEM((w, t), F32)],
        compiler_params=_params("arbitrary", "arbitrary", "arbitrary"),
        name="forgetting_attention",
    )(proj, proj, proj, cum)


DIFF_PACK = 2


def _diff_attn_kernel(q_ref, k_ref, v_ref, lam_ref, subln_ref, o_ref, vt_ref, m_ref, l_ref, acc_ref,
                      *, t, lam_init):
    qi = pl.program_id(2)
    d = HEAD_DIM
    dv = 2 * d
    stats = (m_ref, l_ref, acc_ref)
    n_streams = 2 * DIFF_PACK

    @pl.when(qi == 0)
    def _():
        _store_transposed_blocks(v_ref, vt_ref, t)

    _flash_reset(stats)
    qs = [q_ref[:, a * d:(a + 1) * d] for a in range(n_streams)]

    def block(j, mask):
        k0 = pl.multiple_of(j * t, t)
        _flash_update([(qs[a], k_ref[pl.ds(k0, t), a * d:(a + 1) * d], vt_ref[j, (a // 2) * dv:(a // 2 + 1) * dv, :])
                       for a in range(n_streams)], stats, dv, mask)

    def body(j, c):
        block(j, None)
        return c

    lax.fori_loop(0, qi, body, 0)
    block(qi, _causal_mask_t(t))

    lv = lam_ref[...]
    lam = (jnp.exp(jnp.sum(lv[0:1] * lv[1:2], axis=1, keepdims=True))
           - jnp.exp(jnp.sum(lv[2:3] * lv[3:4], axis=1, keepdims=True)) + lam_init)
    for g in range(DIFF_PACK):
        o_t = _flash_result(stats, 2 * g, dv) - lam * _flash_result(stats, 2 * g + 1, dv)
        o_t = o_t * lax.rsqrt(jnp.mean(o_t * o_t, axis=0, keepdims=True) + RMS_EPS)
        o_ref[:, g * dv:(g + 1) * dv] = (o_t.T * subln_ref[...] * (1.0 - lam_init)).astype(BF16)


def _diff_attention(proj, lam_vecs, subln_w, *, batch, seq, heads, lam_init):
    N = proj.shape[0]
    t = _pick(seq, 512)
    nq = seq // t
    dv = 2 * HEAD_DIM
    w = DIFF_PACK * dv
    assert heads % DIFF_PACK == 0
    hb = heads // DIFF_PACK
    return pl.pallas_call(
        functools.partial(_diff_attn_kernel, t=t, lam_init=lam_init),
        grid=(batch, hb, nq),
        in_specs=[pl.BlockSpec((t, w), lambda b, h, i: (b * nq + i, h)),
                  pl.BlockSpec((seq, w), lambda b, h, i: (b, hb + h)),
                  pl.BlockSpec((seq, w), lambda b, h, i: (b, 2 * hb + h)),
                  pl.BlockSpec((4, HEAD_DIM), lambda b, h, i: (0, 0)),
                  pl.BlockSpec((1, dv), lambda b, h, i: (0, 0))],
        out_specs=pl.BlockSpec((t, w), lambda b, h, i: (b * nq + i, h)),
        out_shape=jax.ShapeDtypeStruct((N, heads * dv), BF16),
        scratch_shapes=[pltpu.VMEM((seq // t, w, t), BF16),
                        pltpu.VMEM((2 * DIFF_PACK * SUBLANES, t), F32),
                        pltpu.VMEM((2 * DIFF_PACK * SUBLANES, t), F32),
                        pltpu.VMEM((2 * DIFF_PACK * dv, t), F32)],
        compiler_params=_params("arbitrary", "arbitrary", "arbitrary"),
        name="differential_attention",
    )(proj, proj, proj, lam_vecs, subln_w)


def _pack_halves(y):
    w = y.shape[1] // 2
    lo = lax.bitcast_convert_type(y[:, :w].astype(BF16).astype(F32), U32)
    hi = lax.bitcast_convert_type(y[:, w:].astype(BF16).astype(F32), U32)
    return (lo >> 16) | (hi & jnp.uint32(0xFFFF0000))


def _unpack_halves(u):
    lo = lax.bitcast_convert_type(u << 16, F32)
    hi = lax.bitcast_convert_type(u & jnp.uint32(0xFFFF0000), F32)
    return lo, hi


def _layer_norm(z, g, b):
    mu = jnp.mean(z, axis=1, keepdims=True)
    zc = z - mu
    var = jnp.mean(zc * zc, axis=1, keepdims=True)
    return zc * lax.rsqrt(var + LN_EPS) * g + b


def _outproj_ln_kernel(o_ref, w_ref, x_ref, gate_ref, g_ref, b_ref, sc_ref, sh_ref, xo_ref, hp_ref, *, alpha):
    y = jnp.dot(o_ref[...], w_ref[...], preferred_element_type=F32)
    z = alpha * x_ref[...] + (1.0 + gate_ref[0]) * y
    xn = _layer_norm(z, g_ref[...], b_ref[...])
    xo_ref[...] = xn
    hp_ref[...] = _pack_halves(xn * (1.0 + sc_ref[0]) + sh_ref[0])


def _outproj_ln(o, w_bf, x2d, gate, ln_g, ln_b, scale2, shift2, *, seq, alpha):
    N, D = x2d.shape
    K = o.shape[1]
    tm = _pick(seq, 256)
    per = seq // tm
    bspec = pl.BlockSpec((1, 1, D), lambda i: (i // per, 0, 0))
    vspec = pl.BlockSpec((1, D), lambda i: (0, 0))
    return pl.pallas_call(
        functools.partial(_outproj_ln_kernel, alpha=alpha),
        grid=(N // tm,),
        in_specs=[pl.BlockSpec((tm, K), lambda i: (i, 0)),
                  pl.BlockSpec((K, D), lambda i: (0, 0)),
                  pl.BlockSpec((tm, D), lambda i: (i, 0)),
                  bspec, vspec, vspec, bspec, bspec],
        out_specs=[pl.BlockSpec((tm, D), lambda i: (i, 0)),
                   pl.BlockSpec((tm, D // 2), lambda i: (i, 0))],
        out_shape=[jax.ShapeDtypeStruct((N, D), F32), jax.ShapeDtypeStruct((N, D // 2), U32)],
        compiler_params=_params("arbitrary"),
        name="outproj_layernorm",
    )(o, w_bf, x2d, gate, ln_g, ln_b, scale2, shift2)


def _router_kernel(x_ref, sc_ref, sh_ref, w_ref, b_ref, ids_ref, wcol_ref, cnt_ref, carry_ref,
                   *, tm, groups, experts):
    @pl.when(pl.program_id(0) == 0)
    def _():
        carry_ref[...] = jnp.zeros_like(carry_ref)

    n_exp = groups * experts
    h = x_ref[...] * (1.0 + sc_ref[0]) + sh_ref[0]
    logits = lax.dot_general(w_ref[...], h, NT_DIMS, preferred_element_type=F32, precision=HIGHEST)
    logits = logits + b_ref[:, 0:1]

    gl = logits[0:groups]
    gmax = jnp.max(gl, axis=0, keepdims=True)
    gid = lax.broadcasted_iota(I32, gl.shape, 0).astype(F32)
    gidx = jnp.min(jnp.where(gl == gmax, gid, float(groups)), axis=0, keepdims=True)
    g_w = 1.0 / jnp.sum(jnp.exp(gl - gmax), axis=0, keepdims=True)

    el = logits[SUBLANES:SUBLANES + experts]
    for g in range(1, groups):
        el = jnp.where(gidx == float(g), logits[SUBLANES + g * experts:SUBLANES + (g + 1) * experts], el)
    eid = lax.broadcasted_iota(I32, el.shape, 0).astype(F32)
    v1 = jnp.max(el, axis=0, keepdims=True)
    i1 = jnp.min(jnp.where(el == v1, eid, float(experts)), axis=0, keepdims=True)
    el2 = jnp.where(eid == i1, -jnp.inf, el)
    v2 = jnp.max(el2, axis=0, keepdims=True)
    i2 = jnp.min(jnp.where(el2 == v2, eid, float(experts)), axis=0, keepdims=True)
    t = jnp.exp(v2 - v1)
    p1 = 1.0 / (1.0 + t)
    w1 = g_w * p1
    w2 = g_w * (t * p1)
    e1 = gidx * float(experts) + i1
    e2 = gidx * float(experts) + i2

    xid = lax.broadcasted_iota(I32, (n_exp, tm), 0).astype(F32)
    oh1 = xid == e1
    oh2 = xid == e2
    cnt = jnp.where(oh1, 1.0, 0.0) + jnp.where(oh2, 1.0, 0.0)
    before = (lax.broadcasted_iota(I32, (tm, tm), 0) < lax.broadcasted_iota(I32, (tm, tm), 1))
    prefix = jnp.dot(cnt.astype(BF16), jnp.where(before, 1.0, 0.0).astype(BF16), preferred_element_type=F32)
    base = carry_ref[:, 0:1] + prefix
    r1 = jnp.sum(jnp.where(oh1, base, 0.0), axis=0, keepdims=True)
    r2 = jnp.sum(jnp.where(oh2, base, 0.0), axis=0, keepdims=True)
    carry_ref[...] = carry_ref[...] + jnp.sum(cnt, axis=1, keepdims=True)
    cnt_ref[...] = carry_ref[...]

    ids_ref[...] = jnp.zeros_like(ids_ref)
    ids_ref[0:1, :] = e1.astype(I32)
    ids_ref[1:2, :] = e2.astype(I32)
    ids_ref[2:3, :] = r1.astype(I32)
    ids_ref[3:4, :] = r2.astype(I32)
    rid = lax.broadcasted_iota(I32, (LANES, tm), 0)
    wrows = jnp.where(rid == 0, w1, jnp.where(rid == 1, w2, 0.0))
    wcol_ref[...] = wrows.T


def _router(xn, scale, shift, w_t, b_col, *, seq, groups, experts):
    N, D = xn.shape
    tm = _pick(seq, 512)
    per = seq // tm
    rows = w_t.shape[0]
    n_exp = groups * experts
    bspec = pl.BlockSpec((1, 1, D), lambda i: (i // per, 0, 0))
    return pl.pallas_call(
        functools.partial(_router_kernel, tm=tm, groups=groups, experts=experts),
        grid=(N // tm,),
        in_specs=[pl.BlockSpec((tm, D), lambda i: (i, 0)), bspec, bspec,
                  pl.BlockSpec((rows, D), lambda i: (0, 0)),
                  pl.BlockSpec((rows, LANES), lambda i: (0, 0))],
        out_specs=[pl.BlockSpec((SUBLANES, tm), lambda i: (0, i)),
                   pl.BlockSpec((tm, LANES), lambda i: (i, 0)),
                   pl.BlockSpec((n_exp, LANES), lambda i: (0, 0))],
        out_shape=[jax.ShapeDtypeStruct((SUBLANES, N), I32),
                   jax.ShapeDtypeStruct((N, LANES), F32),
                   jax.ShapeDtypeStruct((n_exp, LANES), F32)],
        scratch_shapes=[pltpu.VMEM((n_exp, LANES), F32)],
        compiler_params=_params("arbitrary"),
        name="moe_router",
    )(xn, scale, shift, w_t, b_col)


def _dispatch_kernel(pos_ref, hp_ref, xs_in_ref, xs_ref, sem, *, tm, n_tok):
    del xs_in_ref
    base = pl.program_id(0) * tm

    def row_copy(r, p):
        return pltpu.make_async_copy(hp_ref.at[pl.ds(r, 1), :], xs_ref.at[pl.ds(p, 1), :], sem)

    def issue(r, c):
        row_copy(r, pos_ref[base + r]).start()
        row_copy(r, pos_ref[n_tok + base + r]).start()
        return c

    def drain(r, c):
        row_copy(0, 0).wait()
        row_copy(0, 0).wait()
        return c

    lax.fori_loop(0, tm, issue, 0)
    lax.fori_loop(0, tm, drain, 0)


def _dispatch(pos, hp, n_slots):
    N, D2 = hp.shape
    tm = _pick(N, 256)
    grid_spec = pltpu.PrefetchScalarGridSpec(
        num_scalar_prefetch=1,
        grid=(N // tm,),
        in_specs=[pl.BlockSpec((tm, D2), lambda i, pos: (i, 0)),
                  pl.BlockSpec(memory_space=pl.ANY)],
        out_specs=pl.BlockSpec(memory_space=pl.ANY),
        scratch_shapes=[pltpu.SemaphoreType.DMA(())],
    )
    return pl.pallas_call(
        functools.partial(_dispatch_kernel, tm=tm, n_tok=N),
        grid_spec=grid_spec,
        out_shape=jax.ShapeDtypeStruct((n_slots, D2), U32),
        input_output_aliases={2: 0},
        compiler_params=_params("arbitrary"),
        name="moe_dispatch",
    )(pos, hp, jnp.zeros((n_slots, D2), U32))


def _expert_kernel(te_ref, tb_ref, tv_ref, xs_ref, wg_ref, wu_ref, wd_ref, ys_ref, wg_bf, wu_bf, wd_bf):
    j = pl.program_id(0)
    prev = te_ref[jnp.maximum(j - 1, 0)]

    @pl.when(jnp.logical_or(j == 0, te_ref[j] != prev))
    def _():
        wg_bf[...] = wg_ref[0].astype(BF16)
        wu_bf[...] = wu_ref[0].astype(BF16)
        wd_bf[...] = wd_ref[0].astype(BF16)

    @pl.when(tv_ref[j] == 1)
    def _():
        lo, hi = _unpack_halves(xs_ref[...])
        x = jnp.concatenate([lo.astype(BF16), hi.astype(BF16)], axis=1)
        a = jnp.dot(x, wg_bf[...], preferred_element_type=F32)
        u = jnp.dot(x, wu_bf[...], preferred_element_type=F32)
        hid = (a * jax.nn.sigmoid(a) * u).astype(BF16)
        ys_ref[...] = _pack_halves(jnp.dot(hid, wd_bf[...], preferred_element_type=F32))


def _experts(tile_expert, tile_block, tile_valid, xs, w_gate, w_up, w_down, *, tile):
    P, D2 = xs.shape
    _, D, F = w_gate.shape
    n_tiles = P // tile
    grid_spec = pltpu.PrefetchScalarGridSpec(
        num_scalar_prefetch=3,
        grid=(n_tiles,),
        in_specs=[pl.BlockSpec((tile, D2), lambda j, te, tb, tv: (tb[j], 0)),
                  pl.BlockSpec((1, D, F), lambda j, te, tb, tv: (te[j], 0, 0)),
                  pl.BlockSpec((1, D, F), lambda j, te, tb, tv: (te[j], 0, 0)),
                  pl.BlockSpec((1, F, D), lambda j, te, tb, tv: (te[j], 0, 0))],
        out_specs=pl.BlockSpec((tile, D2), lambda j, te, tb, tv: (tb[j], 0)),
        scratch_shapes=[pltpu.VMEM((D, F), BF16), pltpu.VMEM((D, F), BF16), pltpu.VMEM((F, D), BF16)],
    )
    return pl.pallas_call(
        _expert_kernel,
        grid_spec=grid_spec,
        out_shape=jax.ShapeDtypeStruct((P, D2), U32),
        input_output_aliases={3: 0},
        compiler_params=_params("arbitrary"),
        name="moe_experts",
    )(tile_expert, tile_block, tile_valid, xs, w_gate, w_up, w_down)


def _combine_ln_kernel(pos_ref, x_ref, wcol_ref, gate_ref, g_ref, b_ref, ys_ref, o_ref, buf, sem,
                       *, tm, n_tok, alpha):
    base = pl.program_id(0) * tm

    def row_copy(k, r, p):
        return pltpu.make_async_copy(ys_ref.at[pl.ds(p, 1), :], buf.at[k, pl.ds(r, 1), :], sem)

    def issue(r, c):
        row_copy(0, r, pos_ref[base + r]).start()
        row_copy(1, r, pos_ref[n_tok + base + r]).start()
        return c

    def drain(r, c):
        row_copy(0, 0, 0).wait()
        row_copy(1, 0, 0).wait()
        return c

    lax.fori_loop(0, tm, issue, 0)
    lax.fori_loop(0, tm, drain, 0)

    w = wcol_ref[...]
    lo1, hi1 = _unpack_halves(buf[0])
    lo2, hi2 = _unpack_halves(buf[1])
    w1 = w[:, 0:1]
    w2 = w[:, 1:2]
    y = jnp.concatenate([w1 * lo1 + w2 * lo2, w1 * hi1 + w2 * hi2], axis=1)
    z = alpha * x_ref[...] + (1.0 + gate_ref[0]) * y
    o_ref[...] = _layer_norm(z, g_ref[...], b_ref[...])


def _combine_ln(pos, xn, wcol, gate, ln_g, ln_b, ys, *, seq, alpha):
    N, D = xn.shape
    D2 = ys.shape[1]
    tm = _pick(seq, 256)
    per = seq // tm
    grid_spec = pltpu.PrefetchScalarGridSpec(
        num_scalar_prefetch=1,
        grid=(N // tm,),
        in_specs=[pl.BlockSpec((tm, D), lambda i, pos: (i, 0)),
                  pl.BlockSpec((tm, LANES), lambda i, pos: (i, 0)),
                  pl.BlockSpec((1, 1, D), lambda i, pos: (i // per, 0, 0)),
                  pl.BlockSpec((1, D), lambda i, pos: (0, 0)),
                  pl.BlockSpec((1, D), lambda i, pos: (0, 0)),
                  pl.BlockSpec(memory_space=pl.ANY)],
        out_specs=pl.BlockSpec((tm, D), lambda i, pos: (i, 0)),
        scratch_shapes=[pltpu.VMEM((TOP_K, tm, D2), U32), pltpu.SemaphoreType.DMA(())],
    )
    return pl.pallas_call(
        functools.partial(_combine_ln_kernel, tm=tm, n_tok=N, alpha=alpha),
        grid_spec=grid_spec,
        out_shape=jax.ShapeDtypeStruct((N, D), F32),
        compiler_params=_params("arbitrary"),
        name="moe_combine_layernorm",
    )(pos, xn, wcol, gate, ln_g, ln_b, ys)


def _moe_sublayer(xn, hp, scale, shift, gate, ln_g, ln_b, w_group, b_group, w_router, b_router,
                  w_gate, w_up, w_down, layer, *, seq, alpha, tile):
    N, D = xn.shape
    G, _, E = w_router.shape
    n_exp = G * E
    w_t = jnp.concatenate([w_group.T, jnp.zeros((SUBLANES - G, D), F32),
                           w_router.transpose(0, 2, 1).reshape(n_exp, D)], axis=0)
    b_col = jnp.concatenate([b_group, jnp.zeros((SUBLANES - G,), F32), b_router.reshape(n_exp)])
    b_col = jnp.broadcast_to(b_col[:, None], (SUBLANES + n_exp, LANES))
    ids, wcol, counts = _router(xn, scale, shift, w_t, b_col, seq=seq, groups=G, experts=E)

    cnt = counts[:, 0].astype(I32)
    tiles_per = (cnt + tile - 1) // tile
    tile_end = jnp.cumsum(tiles_per)
    offs = (tile_end - tiles_per) * tile
    pos = jnp.concatenate([offs[ids[0]] + ids[2], offs[ids[1]] + ids[3]])
    n_tiles = (TOP_K * N) // tile + n_exp
    n_used = tile_end[n_exp - 1]
    jt = jnp.arange(n_tiles, dtype=I32)
    last = jnp.maximum(n_used - 1, 0)
    tile_block = jnp.minimum(jt, last)
    tile_expert = jnp.sum((tile_end[None, :] <= tile_block[:, None]).astype(I32), axis=1)
    tile_valid = (jt < n_used).astype(I32)

    xs = _dispatch(pos, hp, n_tiles * tile)
    ys = _experts(tile_expert + layer * n_exp, tile_block, tile_valid, xs, w_gate, w_up, w_down, tile=tile)
    return _combine_ln(pos, xn, wcol, gate, ln_g, ln_b, ys, seq=seq, alpha=alpha)


def _rope_tables(seq):
    inv_freq = jnp.power(ROPE_THETA, -jnp.arange(0, HEAD_DIM, 2, dtype=F32) / HEAD_DIM)
    ang = jnp.arange(seq, dtype=F32)[:, None] * inv_freq[None, :]
    cos, sin = jnp.cos(ang), jnp.sin(ang)
    return jnp.concatenate([cos, cos], axis=1), jnp.concatenate([-sin, sin], axis=1)


def kernel(x, c, mix_mod_w, mix_mod_b, mix_ln_g, mix_ln_b, a_w_in, a_lam_q1, a_lam_k1, a_lam_q2, a_lam_k2, a_subln_w, a_w_out, b_w_in, b_forget_bias, b_w_out, ffn_mod_w, ffn_mod_b, ffn_ln_g, ffn_ln_b, moe_w_group, moe_b_group, moe_w_router, moe_b_router, moe_w_gate, moe_w_up, moe_w_down):
    B, S, D = x.shape
    depth = mix_mod_w.shape[0]
    N = B * S
    alpha = (2.0 * depth) ** 0.25
    da_heads = D // (2 * HEAD_DIM)
    fx_heads = D // HEAD_DIM
    G, E, _, F = moe_w_gate.shape[1:]
    assert B <= SUBLANES and fx_heads <= LANES and G <= SUBLANES

    c_pad = jnp.zeros((SUBLANES, D), F32).at[:B].set(c)
    mix_mod = _modulation(c_pad, mix_mod_w, mix_mod_b)[:, :B]
    ffn_mod = _modulation(c_pad, ffn_mod_w, ffn_mod_b)[:, :B]

    def split(m):
        return tuple(m[:, None, k * D:(k + 1) * D] for k in range(3))

    rope_tabs = _rope_tables(S)
    w_gate_all = moe_w_gate.reshape(depth * G * E, D, F)
    w_up_all = moe_w_up.reshape(depth * G * E, D, F)
    w_down_all = moe_w_down.reshape(depth * G * E, F, D)

    x2d = x.reshape(N, D)
    for i in range(depth):
        j = i // 2
        shift, scale, gate = split(mix_mod[i])
        shift2, scale2, gate2 = split(ffn_mod[i])
        if i % 2 == 0:
            qk_width = 2 * da_heads * HEAD_DIM
            proj = _inproj(x2d, scale, shift, a_w_in[j].astype(BF16), seq=S, q_width=qk_width,
                           rope_tabs=rope_tabs, rope_width=2 * qk_width)
            lam_vecs = jnp.stack([a_lam_q1[j], a_lam_k1[j], a_lam_q2[j], a_lam_k2[j]])
            lam_init = 0.8 - 0.6 * math.exp(-0.3 * i)
            o = _diff_attention(proj, lam_vecs, a_subln_w[j][None, :], batch=B, seq=S, heads=da_heads,
                                lam_init=lam_init)
            w_out = a_w_out[j]
        else:
            width = fx_heads * HEAD_DIM
            w_in = b_w_in[j]
            w_f = jnp.zeros((D, LANES), F32).at[:, :fx_heads].set(w_in[:, 3 * width:])
            proj, f_logit = _inproj(x2d, scale, shift, w_in[:, :3 * width].astype(BF16), seq=S,
                                    q_width=width, w_gate=w_f)
            bias = jnp.zeros((1, LANES), F32).at[0, :fx_heads].set(b_forget_bias[j])
            cum = _gate_cumsum(f_logit, bias, batch=B, seq=S)
            o = _fox_attention(proj, cum, batch=B, seq=S, heads=fx_heads)
            w_out = b_w_out[j]
        xn, hp = _outproj_ln(o, w_out.astype(BF16), x2d, gate, mix_ln_g[i][None, :], mix_ln_b[i][None, :],
                             scale2, shift2, seq=S, alpha=alpha)
        x2d = _moe_sublayer(xn, hp, scale2, shift2, gate2, ffn_ln_g[i][None, :], ffn_ln_b[i][None, :],
                            moe_w_group[i], moe_b_group[i], moe_w_router[i], moe_b_router[i],
                            w_gate_all, w_up_all, w_down_all, i, seq=S, alpha=alpha, tile=256)
    return x2d.reshape(B, S, D)
```

```python
import functools
import math

import jax
import jax.numpy as jnp
from jax import lax
from jax.experimental import pallas as pl
from jax.experimental.pallas import tpu as pltpu

F32 = jnp.float32
BF16 = jnp.bfloat16
U32 = jnp.uint32
I32 = jnp.int32
HIGHEST = lax.Precision.HIGHEST

HEAD_DIM = 128
LANES = 128
SUBLANES = 8
MXU_COLS = 256
GATE_LO_LANE = 64
ROW_UNROLL = 8
ROPE_THETA = 10000.0
LN_EPS = 1e-5
RMS_EPS = 1e-5
TOP_K = 2
VMEM_LIMIT = 56 * 1024 * 1024

NT_DIMS = (((1,), (1,)), ((), ()))
LOG2E = math.log2(math.e)


def _params(*sem):
    return pltpu.CompilerParams(dimension_semantics=sem, vmem_limit_bytes=VMEM_LIMIT)


def _pick(total, want):
    t = min(total, want)
    assert total % t == 0, (total, want)
    return t


def _mod_kernel(c_ref, w_ref, b_ref, o_ref):
    c = c_ref[...]
    act = c * jax.nn.sigmoid(c)
    o_ref[0] = jnp.dot(act, w_ref[0], preferred_element_type=F32, precision=HIGHEST) + b_ref[0]


def _modulation(c_pad, w, b):
    L, D, D3 = w.shape
    tn = _pick(D3, 768)
    return pl.pallas_call(
        _mod_kernel,
        grid=(L, D3 // tn),
        in_specs=[pl.BlockSpec((SUBLANES, D), lambda l, j: (0, 0)),
                  pl.BlockSpec((1, D, tn), lambda l, j: (l, 0, j)),
                  pl.BlockSpec((1, 1, tn), lambda l, j: (l, 0, j))],
        out_specs=pl.BlockSpec((1, SUBLANES, tn), lambda l, j: (l, 0, j)),
        out_shape=jax.ShapeDtypeStruct((L, SUBLANES, D3), F32),
        compiler_params=_params("arbitrary", "arbitrary"),
        name="adaln_modulation",
    )(c_pad, w, b.reshape(L, 1, D3))


def _inproj_kernel(*refs, rope, has_gate, n_q_tiles, n_rope_tiles, q_scale, tn):
    it = iter(refs)
    x_ref, sc_ref, sh_ref, w_ref = next(it), next(it), next(it), next(it)
    cos_ref = sin_ref = wf_ref = f_ref = None
    if rope:
        cos_ref, sin_ref = next(it), next(it)
    if has_gate:
        wf_ref = next(it)
    o_ref = next(it)
    if has_gate:
        f_ref = next(it)
    h_ref = next(it)
    j = pl.program_id(1)

    @pl.when(j == 0)
    def _():
        h = x_ref[...] * (1.0 + sc_ref[0]) + sh_ref[0]
        h_hi = h.astype(BF16)
        h_ref[...] = h_hi
        if has_gate:
            h_lo = (h - h_hi.astype(F32)).astype(BF16)
            f = jnp.dot(h_hi, wf_ref[0], preferred_element_type=F32)
            f = f + pltpu.roll(f, LANES - GATE_LO_LANE, 1) + jnp.dot(h_lo, wf_ref[1], preferred_element_type=F32)
            f_ref[...] = f

    mult = jnp.where(j < n_q_tiles, jnp.float32(q_scale), jnp.float32(1.0))
    if rope:
        rotary = j < n_rope_tiles
        cos = jnp.where(rotary, cos_ref[...], 1.0) * mult
        sin = jnp.where(rotary, sin_ref[...], 0.0) * mult
    chunk = min(tn, MXU_COLS)
    for c in range(tn // chunk):
        o = jnp.dot(h_ref[...], w_ref[:, c * chunk:(c + 1) * chunk], preferred_element_type=F32)
        if rope:
            for k in range(chunk // HEAD_DIM):
                blk = o[:, k * HEAD_DIM:(k + 1) * HEAD_DIM]
                rot = pltpu.roll(blk, HEAD_DIM // 2, 1)
                col = c * chunk + k * HEAD_DIM
                o_ref[:, col:col + HEAD_DIM] = (blk * cos + rot * sin).astype(BF16)
        else:
            o_ref[:, c * chunk:(c + 1) * chunk] = (o * mult).astype(BF16)


def _inproj(x2d, scale, shift, w_bf, *, seq, q_width, rope_tabs=None, rope_width=0, w_gate=None):
    N, D = x2d.shape
    W = w_bf.shape[1]
    tm = _pick(seq, 512)
    tn = _pick(q_width, 1024)
    assert W % tn == 0 and rope_width % tn == 0
    tiles_per_seq = seq // tm
    rope = rope_tabs is not None
    has_gate = w_gate is not None
    in_specs = [pl.BlockSpec((tm, D), lambda i, j: (i, 0)),
                pl.BlockSpec((1, 1, D), lambda i, j: (i // tiles_per_seq, 0, 0)),
                pl.BlockSpec((1, 1, D), lambda i, j: (i // tiles_per_seq, 0, 0)),
                pl.BlockSpec((D, tn), lambda i, j: (0, j))]
    args = [x2d, scale, shift, w_bf]
    if rope:
        tab_spec = pl.BlockSpec((tm, HEAD_DIM), lambda i, j: (i % tiles_per_seq, 0))
        in_specs += [tab_spec, tab_spec]
        args += list(rope_tabs)
    out_specs = [pl.BlockSpec((tm, tn), lambda i, j: (i, j))]
    out_shape = [jax.ShapeDtypeStruct((N, W), BF16)]
    if has_gate:
        in_specs.append(pl.BlockSpec((2, D, LANES), lambda i, j: (0, 0, 0)))
        args.append(w_gate)
        out_specs.append(pl.BlockSpec((tm, LANES), lambda i, j: (i, 0)))
        out_shape.append(jax.ShapeDtypeStruct((N, LANES), F32))
    kern = functools.partial(_inproj_kernel, rope=rope, has_gate=has_gate, n_q_tiles=q_width // tn,
                             n_rope_tiles=rope_width // tn, q_scale=HEAD_DIM ** -0.5 * LOG2E, tn=tn)
    outs = pl.pallas_call(
        kern,
        grid=(N // tm, W // tn),
        in_specs=in_specs,
        out_specs=out_specs,
        out_shape=out_shape,
        scratch_shapes=[pltpu.VMEM((tm, D), BF16)],
        compiler_params=_params("arbitrary", "arbitrary"),
        name="modulate_inproj",
    )(*args)
    return tuple(outs) if has_gate else outs[0]


def _gate_cumsum_kernel(f_ref, b_ref, o_ref, carry_ref, *, tc):
    @pl.when(pl.program_id(1) == 0)
    def _():
        carry_ref[...] = jnp.zeros_like(carry_ref)

    z = f_ref[...] + b_ref[...]
    log_f = jnp.minimum(z, 0.0) - jnp.log1p(jnp.exp(-jnp.abs(z)))
    row = lax.broadcasted_iota(I32, (tc, tc), 0)
    col = lax.broadcasted_iota(I32, (tc, tc), 1)
    tri = jnp.where(row >= col, 1.0, 0.0).astype(F32)
    cum = jnp.dot(tri, log_f, preferred_element_type=F32, precision=HIGHEST) + carry_ref[0:1, :]
    o_ref[...] = cum
    carry_ref[...] = jnp.broadcast_to(cum[tc - 1:tc, :], carry_ref.shape)


def _gate_cumsum(f_logit, bias_pad, *, batch, seq):
    N = f_logit.shape[0]
    tc = _pick(seq, 512)
    per = seq // tc
    return pl.pallas_call(
        functools.partial(_gate_cumsum_kernel, tc=tc),
        grid=(batch, per),
        in_specs=[pl.BlockSpec((tc, LANES), lambda b, i: (b * per + i, 0)),
                  pl.BlockSpec((1, LANES), lambda b, i: (0, 0))],
        out_specs=pl.BlockSpec((tc, LANES), lambda b, i: (b * per + i, 0)),
        out_shape=jax.ShapeDtypeStruct((N, LANES), F32),
        scratch_shapes=[pltpu.VMEM((SUBLANES, LANES), F32)],
        compiler_params=_params("arbitrary", "arbitrary"),
        name="forget_gate_cumsum",
    )(f_logit, bias_pad)


def _flash_update(streams, stats, dv, mask):
    m_ref, l_ref, acc_ref = stats
    scores = [lax.dot_general(k, q, NT_DIMS, preferred_element_type=F32) for q, k, _ in streams]
    probs, alphas = [], []
    for a, s in enumerate(scores):
        r = slice(SUBLANES * a, SUBLANES * a + 1)
        if mask is not None:
            s = jnp.where(mask, s, -jnp.inf)
        m_old = m_ref[r, :]
        m_new = jnp.maximum(m_old, jnp.max(s, axis=0, keepdims=True))
        alpha = jnp.exp2(m_old - m_new)
        p = jnp.exp2(s - m_new)
        l_ref[r, :] = alpha * l_ref[r, :] + jnp.sum(p, axis=0, keepdims=True)
        m_ref[r, :] = m_new
        probs.append(p.astype(BF16))
        alphas.append(alpha)
    for a, (_, _, v_t) in enumerate(streams):
        rows = slice(a * dv, (a + 1) * dv)
        acc_ref[rows, :] = alphas[a] * acc_ref[rows, :] + jnp.dot(v_t, probs[a], preferred_element_type=F32)


def _flash_reset(stats):
    m_ref, l_ref, acc_ref = stats
    m_ref[...] = jnp.full(m_ref.shape, -jnp.inf, F32)
    l_ref[...] = jnp.zeros(l_ref.shape, F32)
    acc_ref[...] = jnp.zeros(acc_ref.shape, F32)


def _flash_result(stats, a, dv):
    _, l_ref, acc_ref = stats
    return acc_ref[a * dv:(a + 1) * dv, :] * (1.0 / l_ref[SUBLANES * a:SUBLANES * a + 1, :])


def _causal_mask_t(t):
    return lax.broadcasted_iota(I32, (t, t), 1) >= lax.broadcasted_iota(I32, (t, t), 0)


def _store_transposed_blocks(v_ref, vt_ref, t):
    for jb in range(vt_ref.shape[0]):
        vt_ref[jb] = v_ref[jb * t:(jb + 1) * t, :].astype(F32).T.astype(BF16)


FOX_PACK = 4
N_SPLIT = 3


def _fox_attn_kernel(q_ref, k_ref, v_ref, cum_ref, o_ref, qx_ref, kx_ref, vt_ref, m_ref, l_ref, acc_ref,
                     *, t, seq):
    hp = pl.program_id(1)
    qi = pl.program_id(2)
    d = HEAD_DIM
    stats = (m_ref, l_ref, acc_ref)

    @pl.when(qi == 0)
    def _():
        _store_transposed_blocks(v_ref, vt_ref, t)
        lane = lax.broadcasted_iota(I32, (seq, LANES), 1)
        for a in range(FOX_PACK):
            c = jnp.sum(jnp.where(lane == hp * FOX_PACK + a, cum_ref[...], 0.0), axis=1, keepdims=True)
            rest = jnp.broadcast_to(c * LOG2E, (seq, LANES))
            qx = jnp.where(lane < 2 * N_SPLIT, 1.0, 0.0)
            kx = qx
            for i in range(N_SPLIT):
                piece = rest.astype(BF16).astype(F32)
                rest = rest - piece
                qx = jnp.where(lane == i, piece, qx)
                kx = jnp.where(lane == N_SPLIT + i, -piece, kx)
            qx_ref[a] = qx.astype(BF16)
            kx_ref[a] = kx.astype(BF16)

    _flash_reset(stats)
    q0 = pl.multiple_of(qi * t, t)
    qs = [jnp.concatenate([q_ref[:, a * d:(a + 1) * d], qx_ref[a, pl.ds(q0, t), :]], axis=1)
          for a in range(FOX_PACK)]

    def block(j, mask):
        k0 = pl.multiple_of(j * t, t)
        streams = []
        for a in range(FOX_PACK):
            k = jnp.concatenate([k_ref[pl.ds(k0, t), a * d:(a + 1) * d], kx_ref[a, pl.ds(k0, t), :]], axis=1)
            streams.append((qs[a], k, vt_ref[j, a * d:(a + 1) * d, :]))
        _flash_update(streams, stats, d, mask)

    def body(j, c):
        block(j, None)
        return c

    lax.fori_loop(0, qi, body, 0)
    block(qi, _causal_mask_t(t))
    for a in range(FOX_PACK):
        o_ref[:, a * d:(a + 1) * d] = _flash_result(stats, a, d).T.astype(BF16)


def _fox_attention(proj, cum, *, batch, seq, heads):
    N = proj.shape[0]
    t = _pick(seq, 512)
    nq = seq // t
    w = FOX_PACK * HEAD_DIM
    assert heads % FOX_PACK == 0
    hb = heads // FOX_PACK
    return pl.pallas_call(
        functools.partial(_fox_attn_kernel, t=t, seq=seq),
        grid=(batch, hb, nq),
        in_specs=[pl.BlockSpec((t, w), lambda b, h, i: (b * nq + i, h)),
                  pl.BlockSpec((seq, w), lambda b, h, i: (b, hb + h)),
                  pl.BlockSpec((seq, w), lambda b, h, i: (b, 2 * hb + h)),
                  pl.BlockSpec((seq, LANES), lambda b, h, i: (b, 0))],
        out_specs=pl.BlockSpec((t, w), lambda b, h, i: (b * nq + i, h)),
        out_shape=jax.ShapeDtypeStruct((N, heads * HEAD_DIM), BF16),
        scratch_shapes=[pltpu.VMEM((FOX_PACK, seq, LANES), BF16),
                        pltpu.VMEM((FOX_PACK, seq, LANES), BF16),
                        pltpu.VMEM((seq // t, w, t), BF16),
                        pltpu.VMEM((FOX_PACK * SUBLANES, t), F32),
                        pltpu.VMEM((FOX_PACK * SUBLANES, t), F32),
                        pltpu.VMEM((w, t), F32)],
        compiler_params=_params("arbitrary", "arbitrary", "arbitrary"),
        name="forgetting_attention",
    )(proj, proj, proj, cum)


DIFF_PACK = 2


def _diff_attn_kernel(q_ref, k_ref, v_ref, lam_ref, subln_ref, o_ref, vt_ref, m_ref, l_ref, acc_ref,
                      *, t, lam_init):
    qi = pl.program_id(2)
    d = HEAD_DIM
    dv = 2 * d
    stats = (m_ref, l_ref, acc_ref)
    n_streams = 2 * DIFF_PACK

    @pl.when(qi == 0)
    def _():
        _store_transposed_blocks(v_ref, vt_ref, t)

    _flash_reset(stats)
    qs = [q_ref[:, a * d:(a + 1) * d] for a in range(n_streams)]

    def block(j, mask):
        k0 = pl.multiple_of(j * t, t)
        _flash_update([(qs[a], k_ref[pl.ds(k0, t), a * d:(a + 1) * d], vt_ref[j, (a // 2) * dv:(a // 2 + 1) * dv, :])
                       for a in range(n_streams)], stats, dv, mask)

    def body(j, c):
        block(j, None)
        return c

    lax.fori_loop(0, qi, body, 0)
    block(qi, _causal_mask_t(t))

    lv = lam_ref[...]
    lam = (jnp.exp(jnp.sum(lv[0:1] * lv[1:2], axis=1, keepdims=True))
           - jnp.exp(jnp.sum(lv[2:3] * lv[3:4], axis=1, keepdims=True)) + lam_init)
    for g in range(DIFF_PACK):
        o_t = _flash_result(stats, 2 * g, dv) - lam * _flash_result(stats, 2 * g + 1, dv)
        o_t = o_t * lax.rsqrt(jnp.mean(o_t * o_t, axis=0, keepdims=True) + RMS_EPS)
        o_ref[:, g * dv:(g + 1) * dv] = (o_t.T * subln_ref[...] * (1.0 - lam_init)).astype(BF16)


def _diff_attention(proj, lam_vecs, subln_w, *, batch, seq, heads, lam_init):
    N = proj.shape[0]
    t = _pick(seq, 512)
    nq = seq // t
    dv = 2 * HEAD_DIM
    w = DIFF_PACK * dv
    assert heads % DIFF_PACK == 0
    hb = heads // DIFF_PACK
    return pl.pallas_call(
        functools.partial(_diff_attn_kernel, t=t, lam_init=lam_init),
        grid=(batch, hb, nq),
        in_specs=[pl.BlockSpec((t, w), lambda b, h, i: (b * nq + i, h)),
                  pl.BlockSpec((seq, w), lambda b, h, i: (b, hb + h)),
                  pl.BlockSpec((seq, w), lambda b, h, i: (b, 2 * hb + h)),
                  pl.BlockSpec((4, HEAD_DIM), lambda b, h, i: (0, 0)),
                  pl.BlockSpec((1, dv), lambda b, h, i: (0, 0))],
        out_specs=pl.BlockSpec((t, w), lambda b, h, i: (b * nq + i, h)),
        out_shape=jax.ShapeDtypeStruct((N, heads * dv), BF16),
        scratch_shapes=[pltpu.VMEM((seq // t, w, t), BF16),
                        pltpu.VMEM((2 * DIFF_PACK * SUBLANES, t), F32),
                        pltpu.VMEM((2 * DIFF_PACK * SUBLANES, t), F32),
                        pltpu.VMEM((2 * DIFF_PACK * dv, t), F32)],
        compiler_params=_params("arbitrary", "arbitrary", "arbitrary"),
        name="differential_attention",
    )(proj, proj, proj, lam_vecs, subln_w)


def _pack_halves(y):
    w = y.shape[1] // 2
    lo = lax.bitcast_convert_type(y[:, :w].astype(BF16).astype(F32), U32)
    hi = lax.bitcast_convert_type(y[:, w:].astype(BF16).astype(F32), U32)
    return (lo >> 16) | (hi & jnp.uint32(0xFFFF0000))


def _unpack_halves(u):
    lo = lax.bitcast_convert_type(u << 16, F32)
    hi = lax.bitcast_convert_type(u & jnp.uint32(0xFFFF0000), F32)
    return lo, hi


def _layer_norm(z, g, b):
    mu = jnp.mean(z, axis=1, keepdims=True)
    zc = z - mu
    var = jnp.mean(zc * zc, axis=1, keepdims=True)
    return zc * lax.rsqrt(var + LN_EPS) * g + b


def _outproj_ln_kernel(o_ref, w_ref, x_ref, gate_ref, g_ref, b_ref, sc_ref, sh_ref, xo_ref, hp_ref, *, alpha):
    half = o_ref.shape[0] // 2
    for r in range(2):
        rows = slice(r * half, (r + 1) * half)
        y = jnp.dot(o_ref[rows, :], w_ref[...], preferred_element_type=F32)
        z = alpha * x_ref[rows, :] + (1.0 + gate_ref[0]) * y
        xn = _layer_norm(z, g_ref[...], b_ref[...])
        xo_ref[rows, :] = xn
        hp_ref[rows, :] = _pack_halves(xn * (1.0 + sc_ref[0]) + sh_ref[0])


def _outproj_ln(o, w_bf, x2d, gate, ln_g, ln_b, scale2, shift2, *, seq, alpha):
    N, D = x2d.shape
    K = o.shape[1]
    tm = _pick(seq, 512)
    per = seq // tm
    bspec = pl.BlockSpec((1, 1, D), lambda i: (i // per, 0, 0))
    vspec = pl.BlockSpec((1, D), lambda i: (0, 0))
    return pl.pallas_call(
        functools.partial(_outproj_ln_kernel, alpha=alpha),
        grid=(N // tm,),
        in_specs=[pl.BlockSpec((tm, K), lambda i: (i, 0)),
                  pl.BlockSpec((K, D), lambda i: (0, 0)),
                  pl.BlockSpec((tm, D), lambda i: (i, 0)),
                  bspec, vspec, vspec, bspec, bspec],
        out_specs=[pl.BlockSpec((tm, D), lambda i: (i, 0)),
                   pl.BlockSpec((tm, D // 2), lambda i: (i, 0))],
        out_shape=[jax.ShapeDtypeStruct((N, D), F32), jax.ShapeDtypeStruct((N, D // 2), U32)],
        compiler_params=_params("arbitrary"),
        name="outproj_layernorm",
    )(o, w_bf, x2d, gate, ln_g, ln_b, scale2, shift2)


def _router_kernel(x_ref, sc_ref, sh_ref, w_ref, b_ref, ids_ref, wcol_ref, cnt_ref, carry_ref,
                   *, tm, groups, experts):
    @pl.when(pl.program_id(0) == 0)
    def _():
        carry_ref[...] = jnp.zeros_like(carry_ref)

    n_exp = groups * experts
    h = x_ref[...] * (1.0 + sc_ref[0]) + sh_ref[0]
    logits = lax.dot_general(w_ref[...], h, NT_DIMS, preferred_element_type=F32, precision=HIGHEST)
    logits = logits + b_ref[:, 0:1]

    gl = logits[0:groups]
    gmax = jnp.max(gl, axis=0, keepdims=True)
    gid = lax.broadcasted_iota(I32, gl.shape, 0).astype(F32)
    gidx = jnp.min(jnp.where(gl == gmax, gid, float(groups)), axis=0, keepdims=True)
    g_w = 1.0 / jnp.sum(jnp.exp(gl - gmax), axis=0, keepdims=True)

    el = logits[SUBLANES:SUBLANES + experts]
    for g in range(1, groups):
        el = jnp.where(gidx == float(g), logits[SUBLANES + g * experts:SUBLANES + (g + 1) * experts], el)
    eid = lax.broadcasted_iota(I32, el.shape, 0).astype(F32)
    v1 = jnp.max(el, axis=0, keepdims=True)
    i1 = jnp.min(jnp.where(el == v1, eid, float(experts)), axis=0, keepdims=True)
    el2 = jnp.where(eid == i1, -jnp.inf, el)
    v2 = jnp.max(el2, axis=0, keepdims=True)
    i2 = jnp.min(jnp.where(el2 == v2, eid, float(experts)), axis=0, keepdims=True)
    t = jnp.exp(v2 - v1)
    p1 = 1.0 / (1.0 + t)
    w1 = g_w * p1
    w2 = g_w * (t * p1)
    e1 = gidx * float(experts) + i1
    e2 = gidx * float(experts) + i2

    xid = lax.broadcasted_iota(I32, (n_exp, tm), 0).astype(F32)
    oh1 = xid == e1
    oh2 = xid == e2
    cnt = jnp.where(oh1, 1.0, 0.0) + jnp.where(oh2, 1.0, 0.0)
    before = (lax.broadcasted_iota(I32, (tm, tm), 0) < lax.broadcasted_iota(I32, (tm, tm), 1))
    prefix = jnp.dot(cnt.astype(BF16), jnp.where(before, 1.0, 0.0).astype(BF16), preferred_element_type=F32)
    base = carry_ref[:, 0:1] + prefix
    r1 = jnp.sum(jnp.where(oh1, base, 0.0), axis=0, keepdims=True)
    r2 = jnp.sum(jnp.where(oh2, base, 0.0), axis=0, keepdims=True)
    carry_ref[...] = carry_ref[...] + jnp.sum(cnt, axis=1, keepdims=True)
    cnt_ref[...] = carry_ref[...]

    ids_ref[...] = jnp.zeros_like(ids_ref)
    ids_ref[0:1, :] = e1.astype(I32)
    ids_ref[1:2, :] = e2.astype(I32)
    ids_ref[2:3, :] = r1.astype(I32)
    ids_ref[3:4, :] = r2.astype(I32)
    rid = lax.broadcasted_iota(I32, (LANES, tm), 0)
    wrows = jnp.where(rid == 0, w1, jnp.where(rid == 1, w2, 0.0))
    wcol_ref[...] = wrows.T


def _router(xn, scale, shift, w_t, b_col, *, seq, groups, experts):
    N, D = xn.shape
    tm = _pick(seq, 512)
    per = seq // tm
    rows = w_t.shape[0]
    n_exp = groups * experts
    bspec = pl.BlockSpec((1, 1, D), lambda i: (i // per, 0, 0))
    return pl.pallas_call(
        functools.partial(_router_kernel, tm=tm, groups=groups, experts=experts),
        grid=(N // tm,),
        in_specs=[pl.BlockSpec((tm, D), lambda i: (i, 0)), bspec, bspec,
                  pl.BlockSpec((rows, D), lambda i: (0, 0)),
                  pl.BlockSpec((rows, LANES), lambda i: (0, 0))],
        out_specs=[pl.BlockSpec((SUBLANES, tm), lambda i: (0, i)),
                   pl.BlockSpec((tm, LANES), lambda i: (i, 0)),
                   pl.BlockSpec((n_exp, LANES), lambda i: (0, 0))],
        out_shape=[jax.ShapeDtypeStruct((SUBLANES, N), I32),
                   jax.ShapeDtypeStruct((N, LANES), F32),
                   jax.ShapeDtypeStruct((n_exp, LANES), F32)],
        scratch_shapes=[pltpu.VMEM((n_exp, LANES), F32)],
        compiler_params=_params("arbitrary"),
        name="moe_router",
    )(xn, scale, shift, w_t, b_col)


def _dispatch_kernel(pos_ref, hp_ref, xs_in_ref, xs_ref, sem, *, tm, n_tok):
    del xs_in_ref
    base = pl.program_id(0) * tm

    def issue(g, c):
        for u in range(ROW_UNROLL):
            r = g * ROW_UNROLL + u
            for k in range(TOP_K):
                pltpu.make_async_copy(hp_ref.at[pl.ds(r, 1), :],
                                      xs_ref.at[pl.ds(pos_ref[k * n_tok + base + r], 1), :], sem).start()
        return c

    lax.fori_loop(0, tm // ROW_UNROLL, issue, 0)
    for _ in range(TOP_K):
        pltpu.make_async_copy(hp_ref, xs_ref.at[pl.ds(0, tm), :], sem).wait()


def _dispatch(pos, hp, n_slots):
    N, D2 = hp.shape
    tm = _pick(N, 256)
    grid_spec = pltpu.PrefetchScalarGridSpec(
        num_scalar_prefetch=1,
        grid=(N // tm,),
        in_specs=[pl.BlockSpec((tm, D2), lambda i, pos: (i, 0)),
                  pl.BlockSpec(memory_space=pl.ANY)],
        out_specs=pl.BlockSpec(memory_space=pl.ANY),
        scratch_shapes=[pltpu.SemaphoreType.DMA(())],
    )
    return pl.pallas_call(
        functools.partial(_dispatch_kernel, tm=tm, n_tok=N),
        grid_spec=grid_spec,
        out_shape=jax.ShapeDtypeStruct((n_slots, D2), U32),
        input_output_aliases={2: 0},
        compiler_params=_params("arbitrary"),
        name="moe_dispatch",
    )(pos, hp, jnp.zeros((n_slots, D2), U32))


def _expert_kernel(te_ref, tb_ref, tv_ref, xs_ref, wg_ref, wu_ref, wd_ref, ys_ref, wg_bf, wu_bf, wd_bf):
    j = pl.program_id(0)
    prev = te_ref[jnp.maximum(j - 1, 0)]

    @pl.when(jnp.logical_or(j == 0, te_ref[j] != prev))
    def _():
        wg_bf[...] = wg_ref[0].astype(BF16)
        wu_bf[...] = wu_ref[0].astype(BF16)
        wd_bf[...] = wd_ref[0].astype(BF16)

    @pl.when(tv_ref[j] == 1)
    def _():
        lo, hi = _unpack_halves(xs_ref[...])
        x = jnp.concatenate([lo.astype(BF16), hi.astype(BF16)], axis=1)
        a = jnp.dot(x, wg_bf[...], preferred_element_type=F32)
        u = jnp.dot(x, wu_bf[...], preferred_element_type=F32)
        hid = (a * jax.nn.sigmoid(a) * u).astype(BF16)
        ys_ref[...] = _pack_halves(jnp.dot(hid, wd_bf[...], preferred_element_type=F32))


def _experts(tile_expert, tile_block, tile_valid, xs, w_gate, w_up, w_down, *, tile):
    P, D2 = xs.shape
    _, D, F = w_gate.shape
    n_tiles = P // tile
    grid_spec = pltpu.PrefetchScalarGridSpec(
        num_scalar_prefetch=3,
        grid=(n_tiles,),
        in_specs=[pl.BlockSpec((tile, D2), lambda j, te, tb, tv: (tb[j], 0)),
                  pl.BlockSpec((1, D, F), lambda j, te, tb, tv: (te[j], 0, 0)),
                  pl.BlockSpec((1, D, F), lambda j, te, tb, tv: (te[j], 0, 0)),
                  pl.BlockSpec((1, F, D), lambda j, te, tb, tv: (te[j], 0, 0))],
        out_specs=pl.BlockSpec((tile, D2), lambda j, te, tb, tv: (tb[j], 0)),
        scratch_shapes=[pltpu.VMEM((D, F), BF16), pltpu.VMEM((D, F), BF16), pltpu.VMEM((F, D), BF16)],
    )
    return pl.pallas_call(
        _expert_kernel,
        grid_spec=grid_spec,
        out_shape=jax.ShapeDtypeStruct((P, D2), U32),
        input_output_aliases={3: 0},
        compiler_params=_params("arbitrary"),
        name="moe_experts",
    )(tile_expert, tile_block, tile_valid, xs, w_gate, w_up, w_down)


def _combine_ln_kernel(pos_ref, x_ref, wcol_ref, gate_ref, g_ref, b_ref, ys_ref, o_ref, buf, sem,
                       *, tm, n_tok, alpha):
    i = pl.program_id(0)
    last = pl.num_programs(0) - 1
    slot = i % 2
    chunk = ROW_UNROLL * 4
    n_chunks = tm // chunk

    def issue_rows(tile, s, r0):
        base = tile * tm
        for u in range(chunk):
            r = r0 + u
            for k in range(TOP_K):
                pltpu.make_async_copy(ys_ref.at[pl.ds(pos_ref[k * n_tok + base + r], 1), :],
                                      buf.at[s, k, pl.ds(r, 1), :], sem.at[s]).start()

    def normalise_rows(r0):
        rows = pl.ds(r0, chunk)
        w = wcol_ref[rows, :]
        lo1, hi1 = _unpack_halves(buf[slot, 0, rows, :])
        lo2, hi2 = _unpack_halves(buf[slot, 1, rows, :])
        w1 = w[:, 0:1]
        w2 = w[:, 1:2]
        y = jnp.concatenate([w1 * lo1 + w2 * lo2, w1 * hi1 + w2 * hi2], axis=1)
        z = alpha * x_ref[rows, :] + (1.0 + gate_ref[0]) * y
        o_ref[rows, :] = _layer_norm(z, g_ref[...], b_ref[...])

    @pl.when(i == 0)
    def _():
        def first(c, carry):
            issue_rows(0, 0, pl.multiple_of(c * chunk, chunk))
            return carry
        lax.fori_loop(0, n_chunks, first, 0)

    for k in range(TOP_K):
        pltpu.make_async_copy(ys_ref.at[pl.ds(0, tm), :], buf.at[slot, k], sem.at[slot]).wait()

    @pl.when(i < last)
    def _():
        def body(c, carry):
            r0 = pl.multiple_of(c * chunk, chunk)
            issue_rows(i + 1, 1 - slot, r0)
            normalise_rows(r0)
            return carry
        lax.fori_loop(0, n_chunks, body, 0)

    @pl.when(i == last)
    def _():
        def body(c, carry):
            normalise_rows(pl.multiple_of(c * chunk, chunk))
            return carry
        lax.fori_loop(0, n_chunks, body, 0)


def _combine_ln(pos, xn, wcol, gate, ln_g, ln_b, ys, *, seq, alpha):
    N, D = xn.shape
    D2 = ys.shape[1]
    tm = _pick(seq, 256)
    per = seq // tm
    grid_spec = pltpu.PrefetchScalarGridSpec(
        num_scalar_prefetch=1,
        grid=(N // tm,),
        in_specs=[pl.BlockSpec((tm, D), lambda i, pos: (i, 0)),
                  pl.BlockSpec((tm, LANES), lambda i, pos: (i, 0)),
                  pl.BlockSpec((1, 1, D), lambda i, pos: (i // per, 0, 0)),
                  pl.BlockSpec((1, D), lambda i, pos: (0, 0)),
                  pl.BlockSpec((1, D), lambda i, pos: (0, 0)),
                  pl.BlockSpec(memory_space=pl.ANY)],
        out_specs=pl.BlockSpec((tm, D), lambda i, pos: (i, 0)),
        scratch_shapes=[pltpu.VMEM((2, TOP_K, tm, D2), U32), pltpu.SemaphoreType.DMA((2,))],
    )
    return pl.pallas_call(
        functools.partial(_combine_ln_kernel, tm=tm, n_tok=N, alpha=alpha),
        grid_spec=grid_spec,
        out_shape=jax.ShapeDtypeStruct((N, D), F32),
        compiler_params=_params("arbitrary"),
        name="moe_combine_layernorm",
    )(pos, xn, wcol, gate, ln_g, ln_b, ys)


def _moe_sublayer(xn, hp, scale, shift, gate, ln_g, ln_b, w_group, b_group, w_router, b_router,
                  w_gate, w_up, w_down, layer, *, seq, alpha, tile):
    N, D = xn.shape
    G, _, E = w_router.shape
    n_exp = G * E
    w_t = jnp.concatenate([w_group.T, jnp.zeros((SUBLANES - G, D), F32),
                           w_router.transpose(0, 2, 1).reshape(n_exp, D)], axis=0)
    b_col = jnp.concatenate([b_group, jnp.zeros((SUBLANES - G,), F32), b_router.reshape(n_exp)])
    b_col = jnp.broadcast_to(b_col[:, None], (SUBLANES + n_exp, LANES))
    ids, wcol, counts = _router(xn, scale, shift, w_t, b_col, seq=seq, groups=G, experts=E)

    cnt = counts[:, 0].astype(I32)
    tiles_per = (cnt + tile - 1) // tile
    tile_end = jnp.cumsum(tiles_per)
    offs = (tile_end - tiles_per) * tile
    pos = jnp.concatenate([offs[ids[0]] + ids[2], offs[ids[1]] + ids[3]])
    n_tiles = (TOP_K * N) // tile + n_exp
    n_used = tile_end[n_exp - 1]
    jt = jnp.arange(n_tiles, dtype=I32)
    last = jnp.maximum(n_used - 1, 0)
    tile_block = jnp.minimum(jt, last)
    tile_expert = jnp.sum((tile_end[None, :] <= tile_block[:, None]).astype(I32), axis=1)
    tile_valid = (jt < n_used).astype(I32)

    xs = _dispatch(pos, hp, n_tiles * tile)
    ys = _experts(tile_expert + layer * n_exp, tile_block, tile_valid, xs, w_gate, w_up, w_down, tile=tile)
    return _combine_ln(pos, xn, wcol, gate, ln_g, ln_b, ys, seq=seq, alpha=alpha)


def _rope_tables(seq):
    inv_freq = jnp.power(ROPE_THETA, -jnp.arange(0, HEAD_DIM, 2, dtype=F32) / HEAD_DIM)
    ang = jnp.arange(seq, dtype=F32)[:, None] * inv_freq[None, :]
    cos, sin = jnp.cos(ang), jnp.sin(ang)
    return jnp.concatenate([cos, cos], axis=1), jnp.concatenate([-sin, sin], axis=1)


def kernel(x, c, mix_mod_w, mix_mod_b, mix_ln_g, mix_ln_b, a_w_in, a_lam_q1, a_lam_k1, a_lam_q2, a_lam_k2, a_subln_w, a_w_out, b_w_in, b_forget_bias, b_w_out, ffn_mod_w, ffn_mod_b, ffn_ln_g, ffn_ln_b, moe_w_group, moe_b_group, moe_w_router, moe_b_router, moe_w_gate, moe_w_up, moe_w_down):
    B, S, D = x.shape
    depth = mix_mod_w.shape[0]
    N = B * S
    alpha = (2.0 * depth) ** 0.25
    da_heads = D // (2 * HEAD_DIM)
    fx_heads = D // HEAD_DIM
    G, E, _, F = moe_w_gate.shape[1:]
    assert B <= SUBLANES and fx_heads <= GATE_LO_LANE and G <= SUBLANES

    c_pad = jnp.zeros((SUBLANES, D), F32).at[:B].set(c)
    mix_mod = _modulation(c_pad, mix_mod_w, mix_mod_b)[:, :B]
    ffn_mod = _modulation(c_pad, ffn_mod_w, ffn_mod_b)[:, :B]

    def split(m):
        return tuple(m[:, None, k * D:(k + 1) * D] for k in range(3))

    rope_tabs = _rope_tables(S)
    w_gate_all = moe_w_gate.reshape(depth * G * E, D, F)
    w_up_all = moe_w_up.reshape(depth * G * E, D, F)
    w_down_all = moe_w_down.reshape(depth * G * E, F, D)

    x2d = x.reshape(N, D)
    for i in range(depth):
        j = i // 2
        shift, scale, gate = split(mix_mod[i])
        shift2, scale2, gate2 = split(ffn_mod[i])
        if i % 2 == 0:
            qk_width = 2 * da_heads * HEAD_DIM
            proj = _inproj(x2d, scale, shift, a_w_in[j].astype(BF16), seq=S, q_width=qk_width,
                           rope_tabs=rope_tabs, rope_width=2 * qk_width)
            lam_vecs = jnp.stack([a_lam_q1[j], a_lam_k1[j], a_lam_q2[j], a_lam_k2[j]])
            lam_init = 0.8 - 0.6 * math.exp(-0.3 * i)
            o = _diff_attention(proj, lam_vecs, a_subln_w[j][None, :], batch=B, seq=S, heads=da_heads,
                                lam_init=lam_init)
            w_out = a_w_out[j]
        else:
            width = fx_heads * HEAD_DIM
            w_in = b_w_in[j]
            wf_hi = w_in[:, 3 * width:].astype(BF16)
            wf_lo = (w_in[:, 3 * width:] - wf_hi.astype(F32)).astype(BF16)
            w_f = jnp.zeros((2, D, LANES), BF16)
            w_f = w_f.at[:, :, :fx_heads].set(wf_hi).at[0, :, GATE_LO_LANE:GATE_LO_LANE + fx_heads].set(wf_lo)
            proj, f_logit = _inproj(x2d, scale, shift, w_in[:, :3 * width].astype(BF16), seq=S,
                                    q_width=width, w_gate=w_f)
            bias = jnp.zeros((1, LANES), F32).at[0, :fx_heads].set(b_forget_bias[j])
            cum = _gate_cumsum(f_logit, bias, batch=B, seq=S)
            o = _fox_attention(proj, cum, batch=B, seq=S, heads=fx_heads)
            w_out = b_w_out[j]
        xn, hp = _outproj_ln(o, w_out.astype(BF16), x2d, gate, mix_ln_g[i][None, :], mix_ln_b[i][None, :],
                             scale2, shift2, seq=S, alpha=alpha)
        x2d = _moe_sublayer(xn, hp, scale2, shift2, gate2, ffn_ln_g[i][None, :], ffn_ln_b[i][None, :],
                            moe_w_group[i], moe_b_group[i], moe_w_router[i], moe_b_router[i],
                            w_gate_all, w_up_all, w_down_all, i, seq=S, alpha=alpha, tile=256)
    return x2d.reshape(B, S, D)
```

```python
import functools
import math

import jax
import jax.numpy as jnp
from jax import lax
from jax.experimental import pallas as pl
from jax.experimental.pallas import tpu as pltpu

F32 = jnp.float32
BF16 = jnp.bfloat16
U32 = jnp.uint32
I32 = jnp.int32
HIGHEST = lax.Precision.HIGHEST

HEAD_DIM = 128
LANES = 128
SUBLANES = 8
MXU_COLS = 256
GATE_LO_LANE = 64
ROW_UNROLL = 8
ROPE_THETA = 10000.0
LN_EPS = 1e-5
RMS_EPS = 1e-5
TOP_K = 2
VMEM_LIMIT = 56 * 1024 * 1024

NT_DIMS = (((1,), (1,)), ((), ()))
LOG2E = math.log2(math.e)


def _params(*sem):
    return pltpu.CompilerParams(dimension_semantics=sem, vmem_limit_bytes=VMEM_LIMIT)


def _pick(total, want):
    t = min(total, want)
    assert total % t == 0, (total, want)
    return t


def _mod_kernel(c_ref, w_ref, b_ref, o_ref):
    c = c_ref[...]
    act = c * jax.nn.sigmoid(c)
    o_ref[0] = jnp.dot(act, w_ref[0], preferred_element_type=F32, precision=HIGHEST) + b_ref[0]


def _modulation(c_pad, w, b):
    L, D, D3 = w.shape
    tn = _pick(D3, 768)
    return pl.pallas_call(
        _mod_kernel,
        grid=(L, D3 // tn),
        in_specs=[pl.BlockSpec((SUBLANES, D), lambda l, j: (0, 0)),
                  pl.BlockSpec((1, D, tn), lambda l, j: (l, 0, j)),
                  pl.BlockSpec((1, 1, tn), lambda l, j: (l, 0, j))],
        out_specs=pl.BlockSpec((1, SUBLANES, tn), lambda l, j: (l, 0, j)),
        out_shape=jax.ShapeDtypeStruct((L, SUBLANES, D3), F32),
        compiler_params=_params("arbitrary", "arbitrary"),
        name="adaln_modulation",
    )(c_pad, w, b.reshape(L, 1, D3))


def _inproj_kernel(*refs, rope, has_gate, n_q_tiles, n_rope_tiles, q_scale, tn):
    it = iter(refs)
    x_ref, sc_ref, sh_ref, w_ref = next(it), next(it), next(it), next(it)
    cos_ref = sin_ref = wf_ref = f_ref = None
    if rope:
        cos_ref, sin_ref = next(it), next(it)
    if has_gate:
        wf_ref = next(it)
    o_ref = next(it)
    if has_gate:
        f_ref = next(it)
    h_ref = next(it)
    j = pl.program_id(1)

    @pl.when(j == 0)
    def _():
        h = x_ref[...] * (1.0 + sc_ref[0]) + sh_ref[0]
        h_hi = h.astype(BF16)
        h_ref[...] = h_hi
        if has_gate:
            h_lo = (h - h_hi.astype(F32)).astype(BF16)
            f = jnp.dot(h_hi, wf_ref[0], preferred_element_type=F32)
            f = f + pltpu.roll(f, LANES - GATE_LO_LANE, 1) + jnp.dot(h_lo, wf_ref[1], preferred_element_type=F32)
            f_ref[...] = f

    mult = jnp.where(j < n_q_tiles, jnp.float32(q_scale), jnp.float32(1.0))
    if rope:
        rotary = j < n_rope_tiles
        cos = jnp.where(rotary, cos_ref[...], 1.0) * mult
        sin = jnp.where(rotary, sin_ref[...], 0.0) * mult
    chunk = min(tn, MXU_COLS)
    for c in range(tn // chunk):
        o = jnp.dot(h_ref[...], w_ref[:, c * chunk:(c + 1) * chunk], preferred_element_type=F32)
        if rope:
            for k in range(chunk // HEAD_DIM):
                blk = o[:, k * HEAD_DIM:(k + 1) * HEAD_DIM]
                rot = pltpu.roll(blk, HEAD_DIM // 2, 1)
                col = c * chunk + k * HEAD_DIM
                o_ref[:, col:col + HEAD_DIM] = (blk * cos + rot * sin).astype(BF16)
        else:
            o_ref[:, c * chunk:(c + 1) * chunk] = (o * mult).astype(BF16)


def _inproj(x2d, scale, shift, w_bf, *, seq, q_width, rope_tabs=None, rope_width=0, w_gate=None):
    N, D = x2d.shape
    W = w_bf.shape[1]
    tm = _pick(seq, 512)
    tn = _pick(q_width, 1024)
    assert W % tn == 0 and rope_width % tn == 0
    tiles_per_seq = seq // tm
    rope = rope_tabs is not None
    has_gate = w_gate is not None
    in_specs = [pl.BlockSpec((tm, D), lambda i, j: (i, 0)),
                pl.BlockSpec((1, 1, D), lambda i, j: (i // tiles_per_seq, 0, 0)),
                pl.BlockSpec((1, 1, D), lambda i, j: (i // tiles_per_seq, 0, 0)),
                pl.BlockSpec((D, tn), lambda i, j: (0, j))]
    args = [x2d, scale, shift, w_bf]
    if rope:
        tab_spec = pl.BlockSpec((tm, HEAD_DIM), lambda i, j: (i % tiles_per_seq, 0))
        in_specs += [tab_spec, tab_spec]
        args += list(rope_tabs)
    out_specs = [pl.BlockSpec((tm, tn), lambda i, j: (i, j))]
    out_shape = [jax.ShapeDtypeStruct((N, W), BF16)]
    if has_gate:
        in_specs.append(pl.BlockSpec((2, D, LANES), lambda i, j: (0, 0, 0)))
        args.append(w_gate)
        out_specs.append(pl.BlockSpec((tm, LANES), lambda i, j: (i, 0)))
        out_shape.append(jax.ShapeDtypeStruct((N, LANES), F32))
    kern = functools.partial(_inproj_kernel, rope=rope, has_gate=has_gate, n_q_tiles=q_width // tn,
                             n_rope_tiles=rope_width // tn, q_scale=HEAD_DIM ** -0.5 * LOG2E, tn=tn)
    outs = pl.pallas_call(
        kern,
        grid=(N // tm, W // tn),
        in_specs=in_specs,
        out_specs=out_specs,
        out_shape=out_shape,
        scratch_shapes=[pltpu.VMEM((tm, D), BF16)],
        compiler_params=_params("arbitrary", "arbitrary"),
        name="modulate_inproj",
    )(*args)
    return tuple(outs) if has_gate else outs[0]


def _gate_cumsum_kernel(f_ref, b_ref, o_ref, carry_ref, *, tc):
    @pl.when(pl.program_id(1) == 0)
    def _():
        carry_ref[...] = jnp.zeros_like(carry_ref)

    z = f_ref[...] + b_ref[...]
    log_f = jnp.minimum(z, 0.0) - jnp.log1p(jnp.exp(-jnp.abs(z)))
    row = lax.broadcasted_iota(I32, (tc, tc), 0)
    col = lax.broadcasted_iota(I32, (tc, tc), 1)
    tri = jnp.where(row >= col, 1.0, 0.0).astype(F32)
    cum = jnp.dot(tri, log_f, preferred_element_type=F32, precision=HIGHEST) + carry_ref[0:1, :]
    o_ref[...] = cum
    carry_ref[...] = jnp.broadcast_to(cum[tc - 1:tc, :], carry_ref.shape)


def _gate_cumsum(f_logit, bias_pad, *, batch, seq):
    N = f_logit.shape[0]
    tc = _pick(seq, 512)
    per = seq // tc
    return pl.pallas_call(
        functools.partial(_gate_cumsum_kernel, tc=tc),
        grid=(batch, per),
        in_specs=[pl.BlockSpec((tc, LANES), lambda b, i: (b * per + i, 0)),
                  pl.BlockSpec((1, LANES), lambda b, i: (0, 0))],
        out_specs=pl.BlockSpec((tc, LANES), lambda b, i: (b * per + i, 0)),
        out_shape=jax.ShapeDtypeStruct((N, LANES), F32),
        scratch_shapes=[pltpu.VMEM((SUBLANES, LANES), F32)],
        compiler_params=_params("arbitrary", "arbitrary"),
        name="forget_gate_cumsum",
    )(f_logit, bias_pad)


def _flash_sweep(qi, t, n_streams, get_qk, get_vt, s_ref, stats, dv):
    m_ref, l_ref, acc_ref = stats

    def scores(j, slot):
        for a in range(n_streams):
            q, k = get_qk(j, a)
            s_ref[slot, a] = lax.dot_general(k, q, NT_DIMS, preferred_element_type=F32)

    def consume(j, slot, mask):
        probs, alphas = [], []
        for a in range(n_streams):
            r = slice(SUBLANES * a, SUBLANES * a + 1)
            s = s_ref[slot, a]
            if mask is not None:
                s = jnp.where(mask, s, -jnp.inf)
            m_old = m_ref[r, :]
            m_new = jnp.maximum(m_old, jnp.max(s, axis=0, keepdims=True))
            alpha = jnp.exp2(m_old - m_new)
            p = jnp.exp2(s - m_new)
            l_ref[r, :] = alpha * l_ref[r, :] + jnp.sum(p, axis=0, keepdims=True)
            m_ref[r, :] = m_new
            probs.append(p.astype(BF16))
            alphas.append(alpha)
        for a in range(n_streams):
            rows = slice(a * dv, (a + 1) * dv)
            acc_ref[rows, :] = (alphas[a] * acc_ref[rows, :]
                                + jnp.dot(get_vt(j, a), probs[a], preferred_element_type=F32))

    _flash_reset(stats)
    scores(0, 0)

    def body(jj, c):
        j = 2 * jj
        scores(j + 1, 1)
        consume(j, 0, None)
        scores(j + 2, 0)
        consume(j + 1, 1, None)
        return c

    pairs = qi // 2
    lax.fori_loop(0, pairs, body, 0)

    @pl.when(qi % 2 == 0)
    def _():
        consume(qi, 0, _causal_mask_t(t))

    @pl.when(qi % 2 == 1)
    def _():
        scores(qi, 1)
        consume(qi - 1, 0, None)
        consume(qi, 1, _causal_mask_t(t))


def _flash_reset(stats):
    m_ref, l_ref, acc_ref = stats
    m_ref[...] = jnp.full(m_ref.shape, -jnp.inf, F32)
    l_ref[...] = jnp.zeros(l_ref.shape, F32)
    acc_ref[...] = jnp.zeros(acc_ref.shape, F32)


def _flash_result(stats, a, dv):
    _, l_ref, acc_ref = stats
    return acc_ref[a * dv:(a + 1) * dv, :] * (1.0 / l_ref[SUBLANES * a:SUBLANES * a + 1, :])


def _causal_mask_t(t):
    return lax.broadcasted_iota(I32, (t, t), 1) >= lax.broadcasted_iota(I32, (t, t), 0)


def _store_transposed_blocks(v_ref, vt_ref, t):
    for jb in range(vt_ref.shape[0]):
        vt_ref[jb] = v_ref[jb * t:(jb + 1) * t, :].astype(F32).T.astype(BF16)


FOX_PACK = 4
N_SPLIT = 3


def _fox_attn_kernel(q_ref, k_ref, v_ref, cum_ref, o_ref, qx_ref, kx_ref, vt_ref, s_ref, m_ref, l_ref, acc_ref,
                     *, t, seq):
    hp = pl.program_id(1)
    qi = pl.program_id(2)
    d = HEAD_DIM
    stats = (m_ref, l_ref, acc_ref)

    @pl.when(qi == 0)
    def _():
        _store_transposed_blocks(v_ref, vt_ref, t)
        lane = lax.broadcasted_iota(I32, (seq, LANES), 1)
        for a in range(FOX_PACK):
            c = jnp.sum(jnp.where(lane == hp * FOX_PACK + a, cum_ref[...], 0.0), axis=1, keepdims=True)
            rest = jnp.broadcast_to(c * LOG2E, (seq, LANES))
            qx = jnp.where(lane < 2 * N_SPLIT, 1.0, 0.0)
            kx = qx
            for i in range(N_SPLIT):
                piece = rest.astype(BF16).astype(F32)
                rest = rest - piece
                qx = jnp.where(lane == i, piece, qx)
                kx = jnp.where(lane == N_SPLIT + i, -piece, kx)
            qx_ref[a] = qx.astype(BF16)
            kx_ref[a] = kx.astype(BF16)

    q0 = pl.multiple_of(qi * t, t)

    def get_qk(j, a):
        k0 = pl.multiple_of(j * t, t)
        q = jnp.concatenate([q_ref[:, a * d:(a + 1) * d], qx_ref[a, pl.ds(q0, t), :]], axis=1)
        k = jnp.concatenate([k_ref[pl.ds(k0, t), a * d:(a + 1) * d], kx_ref[a, pl.ds(k0, t), :]], axis=1)
        return q, k

    def get_vt(j, a):
        return vt_ref[j, a * d:(a + 1) * d, :]

    _flash_sweep(qi, t, FOX_PACK, get_qk, get_vt, s_ref, stats, d)
    for a in range(FOX_PACK):
        o_ref[:, a * d:(a + 1) * d] = _flash_result(stats, a, d).T.astype(BF16)


def _fox_attention(proj, cum, *, batch, seq, heads):
    N = proj.shape[0]
    t = _pick(seq, 512)
    nq = seq // t
    w = FOX_PACK * HEAD_DIM
    assert heads % FOX_PACK == 0
    hb = heads // FOX_PACK
    return pl.pallas_call(
        functools.partial(_fox_attn_kernel, t=t, seq=seq),
        grid=(batch, hb, nq),
        in_specs=[pl.BlockSpec((t, w), lambda b, h, i: (b * nq + i, h)),
                  pl.BlockSpec((seq, w), lambda b, h, i: (b, hb + h)),
                  pl.BlockSpec((seq, w), lambda b, h, i: (b, 2 * hb + h)),
                  pl.BlockSpec((seq, LANES), lambda b, h, i: (b, 0))],
        out_specs=pl.BlockSpec((t, w), lambda b, h, i: (b * nq + i, h)),
        out_shape=jax.ShapeDtypeStruct((N, heads * HEAD_DIM), BF16),
        scratch_shapes=[pltpu.VMEM((FOX_PACK, seq, LANES), BF16),
                        pltpu.VMEM((FOX_PACK, seq, LANES), BF16),
                        pltpu.VMEM((seq // t, w, t), BF16),
                        pltpu.VMEM((2, FOX_PACK, t, t), F32),
                        pltpu.VMEM((FOX_PACK * SUBLANES, t), F32),
                        pltpu.VMEM((FOX_PACK * SUBLANES, t), F32),
                        pltpu.VMEM((w, t), F32)],
        compiler_params=_params("arbitrary", "arbitrary", "arbitrary"),
        name="forgetting_attention",
    )(proj, proj, proj, cum)


DIFF_PACK = 2


def _diff_attn_kernel(q_ref, k_ref, v_ref, lam_ref, subln_ref, o_ref, vt_ref, s_ref, m_ref, l_ref, acc_ref,
                      *, t, lam_init):
    qi = pl.program_id(2)
    d = HEAD_DIM
    dv = 2 * d
    stats = (m_ref, l_ref, acc_ref)
    n_streams = 2 * DIFF_PACK

    @pl.when(qi == 0)
    def _():
        _store_transposed_blocks(v_ref, vt_ref, t)

    def get_qk(j, a):
        k0 = pl.multiple_of(j * t, t)
        return q_ref[:, a * d:(a + 1) * d], k_ref[pl.ds(k0, t), a * d:(a + 1) * d]

    def get_vt(j, a):
        return vt_ref[j, (a // 2) * dv:(a // 2 + 1) * dv, :]

    _flash_sweep(qi, t, n_streams, get_qk, get_vt, s_ref, stats, dv)

    lv = lam_ref[...]
    lam = (jnp.exp(jnp.sum(lv[0:1] * lv[1:2], axis=1, keepdims=True))
           - jnp.exp(jnp.sum(lv[2:3] * lv[3:4], axis=1, keepdims=True)) + lam_init)
    for g in range(DIFF_PACK):
        o_t = _flash_result(stats, 2 * g, dv) - lam * _flash_result(stats, 2 * g + 1, dv)
        o_t = o_t * lax.rsqrt(jnp.mean(o_t * o_t, axis=0, keepdims=True) + RMS_EPS)
        o_ref[:, g * dv:(g + 1) * dv] = (o_t.T * subln_ref[...] * (1.0 - lam_init)).astype(BF16)


def _diff_attention(proj, lam_vecs, subln_w, *, batch, seq, heads, lam_init):
    N = proj.shape[0]
    t = _pick(seq, 512)
    nq = seq // t
    dv = 2 * HEAD_DIM
    w = DIFF_PACK * dv
    assert heads % DIFF_PACK == 0
    hb = heads // DIFF_PACK
    return pl.pallas_call(
        functools.partial(_diff_attn_kernel, t=t, lam_init=lam_init),
        grid=(batch, hb, nq),
        in_specs=[pl.BlockSpec((t, w), lambda b, h, i: (b * nq + i, h)),
                  pl.BlockSpec((seq, w), lambda b, h, i: (b, hb + h)),
                  pl.BlockSpec((seq, w), lambda b, h, i: (b, 2 * hb + h)),
                  pl.BlockSpec((4, HEAD_DIM), lambda b, h, i: (0, 0)),
                  pl.BlockSpec((1, dv), lambda b, h, i: (0, 0))],
        out_specs=pl.BlockSpec((t, w), lambda b, h, i: (b * nq + i, h)),
        out_shape=jax.ShapeDtypeStruct((N, heads * dv), BF16),
        scratch_shapes=[pltpu.VMEM((seq // t, w, t), BF16),
                        pltpu.VMEM((2, 2 * DIFF_PACK, t, t), F32),
                        pltpu.VMEM((2 * DIFF_PACK * SUBLANES, t), F32),
                        pltpu.VMEM((2 * DIFF_PACK * SUBLANES, t), F32),
                        pltpu.VMEM((2 * DIFF_PACK * dv, t), F32)],
        compiler_params=_params("arbitrary", "arbitrary", "arbitrary"),
        name="differential_attention",
    )(proj, proj, proj, lam_vecs, subln_w)


def _pack_halves(y):
    w = y.shape[1] // 2
    lo = lax.bitcast_convert_type(y[:, :w].astype(BF16).astype(F32), U32)
    hi = lax.bitcast_convert_type(y[:, w:].astype(BF16).astype(F32), U32)
    return (lo >> 16) | (hi & jnp.uint32(0xFFFF0000))


def _unpack_halves(u):
    lo = lax.bitcast_convert_type(u << 16, F32)
    hi = lax.bitcast_convert_type(u & jnp.uint32(0xFFFF0000), F32)
    return lo, hi


def _layer_norm(z, g, b):
    mu = jnp.mean(z, axis=1, keepdims=True)
    zc = z - mu
    var = jnp.mean(zc * zc, axis=1, keepdims=True)
    return zc * lax.rsqrt(var + LN_EPS) * g + b


def _outproj_ln_kernel(o_ref, w_ref, x_ref, gate_ref, g_ref, b_ref, sc_ref, sh_ref, xo_ref, hp_ref, *, alpha):
    half = o_ref.shape[0] // 2
    for r in range(2):
        rows = slice(r * half, (r + 1) * half)
        y = jnp.dot(o_ref[rows, :], w_ref[...], preferred_element_type=F32)
        z = alpha * x_ref[rows, :] + (1.0 + gate_ref[0]) * y
        xn = _layer_norm(z, g_ref[...], b_ref[...])
        xo_ref[rows, :] = xn
        hp_ref[rows, :] = _pack_halves(xn * (1.0 + sc_ref[0]) + sh_ref[0])


def _outproj_ln(o, w_bf, x2d, gate, ln_g, ln_b, scale2, shift2, *, seq, alpha):
    N, D = x2d.shape
    K = o.shape[1]
    tm = _pick(seq, 512)
    per = seq // tm
    bspec = pl.BlockSpec((1, 1, D), lambda i: (i // per, 0, 0))
    vspec = pl.BlockSpec((1, D), lambda i: (0, 0))
    return pl.pallas_call(
        functools.partial(_outproj_ln_kernel, alpha=alpha),
        grid=(N // tm,),
        in_specs=[pl.BlockSpec((tm, K), lambda i: (i, 0)),
                  pl.BlockSpec((K, D), lambda i: (0, 0)),
                  pl.BlockSpec((tm, D), lambda i: (i, 0)),
                  bspec, vspec, vspec, bspec, bspec],
        out_specs=[pl.BlockSpec((tm, D), lambda i: (i, 0)),
                   pl.BlockSpec((tm, D // 2), lambda i: (i, 0))],
        out_shape=[jax.ShapeDtypeStruct((N, D), F32), jax.ShapeDtypeStruct((N, D // 2), U32)],
        compiler_params=_params("arbitrary"),
        name="outproj_layernorm",
    )(o, w_bf, x2d, gate, ln_g, ln_b, scale2, shift2)


def _router_kernel(x_ref, sc_ref, sh_ref, w_ref, b_ref, ids_ref, wcol_ref, cnt_ref, carry_ref,
                   *, tm, groups, experts):
    @pl.when(pl.program_id(0) == 0)
    def _():
        carry_ref[...] = jnp.zeros_like(carry_ref)

    n_exp = groups * experts
    h = x_ref[...] * (1.0 + sc_ref[0]) + sh_ref[0]
    logits = lax.dot_general(w_ref[...], h, NT_DIMS, preferred_element_type=F32, precision=HIGHEST)
    logits = logits + b_ref[:, 0:1]

    gl = logits[0:groups]
    gmax = jnp.max(gl, axis=0, keepdims=True)
    gid = lax.broadcasted_iota(I32, gl.shape, 0).astype(F32)
    gidx = jnp.min(jnp.where(gl == gmax, gid, float(groups)), axis=0, keepdims=True)
    g_w = 1.0 / jnp.sum(jnp.exp(gl - gmax), axis=0, keepdims=True)

    el = logits[SUBLANES:SUBLANES + experts]
    for g in range(1, groups):
        el = jnp.where(gidx == float(g), logits[SUBLANES + g * experts:SUBLANES + (g + 1) * experts], el)
    eid = lax.broadcasted_iota(I32, el.shape, 0).astype(F32)
    v1 = jnp.max(el, axis=0, keepdims=True)
    i1 = jnp.min(jnp.where(el == v1, eid, float(experts)), axis=0, keepdims=True)
    el2 = jnp.where(eid == i1, -jnp.inf, el)
    v2 = jnp.max(el2, axis=0, keepdims=True)
    i2 = jnp.min(jnp.where(el2 == v2, eid, float(experts)), axis=0, keepdims=True)
    t = jnp.exp(v2 - v1)
    p1 = 1.0 / (1.0 + t)
    w1 = g_w * p1
    w2 = g_w * (t * p1)
    e1 = gidx * float(experts) + i1
    e2 = gidx * float(experts) + i2

    xid = lax.broadcasted_iota(I32, (n_exp, tm), 0).astype(F32)
    oh1 = xid == e1
    oh2 = xid == e2
    cnt = jnp.where(oh1, 1.0, 0.0) + jnp.where(oh2, 1.0, 0.0)
    before = (lax.broadcasted_iota(I32, (tm, tm), 0) < lax.broadcasted_iota(I32, (tm, tm), 1))
    prefix = jnp.dot(cnt.astype(BF16), jnp.where(before, 1.0, 0.0).astype(BF16), preferred_element_type=F32)
    base = carry_ref[:, 0:1] + prefix
    r1 = jnp.sum(jnp.where(oh1, base, 0.0), axis=0, keepdims=True)
    r2 = jnp.sum(jnp.where(oh2, base, 0.0), axis=0, keepdims=True)
    carry_ref[...] = carry_ref[...] + jnp.sum(cnt, axis=1, keepdims=True)
    cnt_ref[...] = carry_ref[...]

    ids_ref[...] = jnp.zeros_like(ids_ref)
    ids_ref[0:1, :] = e1.astype(I32)
    ids_ref[1:2, :] = e2.astype(I32)
    ids_ref[2:3, :] = r1.astype(I32)
    ids_ref[3:4, :] = r2.astype(I32)
    rid = lax.broadcasted_iota(I32, (LANES, tm), 0)
    wrows = jnp.where(rid == 0, w1, jnp.where(rid == 1, w2, 0.0))
    wcol_ref[...] = wrows.T


def _router(xn, scale, shift, w_t, b_col, *, seq, groups, experts):
    N, D = xn.shape
    tm = _pick(seq, 512)
    per = seq // tm
    rows = w_t.shape[0]
    n_exp = groups * experts
    bspec = pl.BlockSpec((1, 1, D), lambda i: (i // per, 0, 0))
    return pl.pallas_call(
        functools.partial(_router_kernel, tm=tm, groups=groups, experts=experts),
        grid=(N // tm,),
        in_specs=[pl.BlockSpec((tm, D), lambda i: (i, 0)), bspec, bspec,
                  pl.BlockSpec((rows, D), lambda i: (0, 0)),
                  pl.BlockSpec((rows, LANES), lambda i: (0, 0))],
        out_specs=[pl.BlockSpec((SUBLANES, tm), lambda i: (0, i)),
                   pl.BlockSpec((tm, LANES), lambda i: (i, 0)),
                   pl.BlockSpec((n_exp, LANES), lambda i: (0, 0))],
        out_shape=[jax.ShapeDtypeStruct((SUBLANES, N), I32),
                   jax.ShapeDtypeStruct((N, LANES), F32),
                   jax.ShapeDtypeStruct((n_exp, LANES), F32)],
        scratch_shapes=[pltpu.VMEM((n_exp, LANES), F32)],
        compiler_params=_params("arbitrary"),
        name="moe_router",
    )(xn, scale, shift, w_t, b_col)


def _dispatch_kernel(pos_ref, hp_ref, xs_in_ref, xs_ref, sem, *, tm, n_tok):
    del xs_in_ref
    base = pl.program_id(0) * tm

    def issue(g, c):
        for u in range(ROW_UNROLL):
            r = g * ROW_UNROLL + u
            for k in range(TOP_K):
                pltpu.make_async_copy(hp_ref.at[pl.ds(r, 1), :],
                                      xs_ref.at[pl.ds(pos_ref[k * n_tok + base + r], 1), :], sem).start()
        return c

    lax.fori_loop(0, tm // ROW_UNROLL, issue, 0)
    for _ in range(TOP_K):
        pltpu.make_async_copy(hp_ref, xs_ref.at[pl.ds(0, tm), :], sem).wait()


def _dispatch(pos, hp, n_slots):
    N, D2 = hp.shape
    tm = _pick(N, 256)
    grid_spec = pltpu.PrefetchScalarGridSpec(
        num_scalar_prefetch=1,
        grid=(N // tm,),
        in_specs=[pl.BlockSpec((tm, D2), lambda i, pos: (i, 0)),
                  pl.BlockSpec(memory_space=pl.ANY)],
        out_specs=pl.BlockSpec(memory_space=pl.ANY),
        scratch_shapes=[pltpu.SemaphoreType.DMA(())],
    )
    return pl.pallas_call(
        functools.partial(_dispatch_kernel, tm=tm, n_tok=N),
        grid_spec=grid_spec,
        out_shape=jax.ShapeDtypeStruct((n_slots, D2), U32),
        input_output_aliases={2: 0},
        compiler_params=_params("arbitrary"),
        name="moe_dispatch",
    )(pos, hp, jnp.zeros((n_slots, D2), U32))


def _expert_kernel(te_ref, tb_ref, tv_ref, xs_ref, wg_ref, wu_ref, wd_ref, ys_ref, wg_bf, wu_bf, wd_bf):
    j = pl.program_id(0)
    prev = te_ref[jnp.maximum(j - 1, 0)]

    @pl.when(jnp.logical_or(j == 0, te_ref[j] != prev))
    def _():
        wg_bf[...] = wg_ref[0].astype(BF16)
        wu_bf[...] = wu_ref[0].astype(BF16)
        wd_bf[...] = wd_ref[0].astype(BF16)

    @pl.when(tv_ref[j] == 1)
    def _():
        lo, hi = _unpack_halves(xs_ref[...])
        x = jnp.concatenate([lo.astype(BF16), hi.astype(BF16)], axis=1)
        a = jnp.dot(x, wg_bf[...], preferred_element_type=F32)
        u = jnp.dot(x, wu_bf[...], preferred_element_type=F32)
        hid = (a * jax.nn.sigmoid(a) * u).astype(BF16)
        ys_ref[...] = _pack_halves(jnp.dot(hid, wd_bf[...], preferred_element_type=F32))


def _experts(tile_expert, tile_block, tile_valid, xs, w_gate, w_up, w_down, *, tile):
    P, D2 = xs.shape
    _, D, F = w_gate.shape
    n_tiles = P // tile
    grid_spec = pltpu.PrefetchScalarGridSpec(
        num_scalar_prefetch=3,
        grid=(n_tiles,),
        in_specs=[pl.BlockSpec((tile, D2), lambda j, te, tb, tv: (tb[j], 0)),
                  pl.BlockSpec((1, D, F), lambda j, te, tb, tv: (te[j], 0, 0)),
                  pl.BlockSpec((1, D, F), lambda j, te, tb, tv: (te[j], 0, 0)),
                  pl.BlockSpec((1, F, D), lambda j, te, tb, tv: (te[j], 0, 0))],
        out_specs=pl.BlockSpec((tile, D2), lambda j, te, tb, tv: (tb[j], 0)),
        scratch_shapes=[pltpu.VMEM((D, F), BF16), pltpu.VMEM((D, F), BF16), pltpu.VMEM((F, D), BF16)],
    )
    return pl.pallas_call(
        _expert_kernel,
        grid_spec=grid_spec,
        out_shape=jax.ShapeDtypeStruct((P, D2), U32),
        input_output_aliases={3: 0},
        compiler_params=_params("arbitrary"),
        name="moe_experts",
    )(tile_expert, tile_block, tile_valid, xs, w_gate, w_up, w_down)


def _combine_ln_kernel(pos_ref, x_ref, wcol_ref, gate_ref, g_ref, b_ref, ys_ref, o_ref, buf, sem,
                       *, tm, n_tok, alpha):
    i = pl.program_id(0)
    last = pl.num_programs(0) - 1
    slot = i % 2
    chunk = ROW_UNROLL * 4
    n_chunks = tm // chunk

    def issue_rows(tile, s, r0):
        base = tile * tm
        for u in range(chunk):
            r = r0 + u
            for k in range(TOP_K):
                pltpu.make_async_copy(ys_ref.at[pl.ds(pos_ref[k * n_tok + base + r], 1), :],
                                      buf.at[s, k, pl.ds(r, 1), :], sem.at[s]).start()

    def normalise_rows(r0):
        rows = pl.ds(r0, chunk)
        w = wcol_ref[rows, :]
        lo1, hi1 = _unpack_halves(buf[slot, 0, rows, :])
        lo2, hi2 = _unpack_halves(buf[slot, 1, rows, :])
        w1 = w[:, 0:1]
        w2 = w[:, 1:2]
        y = jnp.concatenate([w1 * lo1 + w2 * lo2, w1 * hi1 + w2 * hi2], axis=1)
        z = alpha * x_ref[rows, :] + (1.0 + gate_ref[0]) * y
        o_ref[rows, :] = _layer_norm(z, g_ref[...], b_ref[...])

    @pl.when(i == 0)
    def _():
        def first(c, carry):
            issue_rows(0, 0, pl.multiple_of(c * chunk, chunk))
            return carry
        lax.fori_loop(0, n_chunks, first, 0)

    for k in range(TOP_K):
        pltpu.make_async_copy(ys_ref.at[pl.ds(0, tm), :], buf.at[slot, k], sem.at[slot]).wait()

    @pl.when(i < last)
    def _():
        def body(c, carry):
            r0 = pl.multiple_of(c * chunk, chunk)
            issue_rows(i + 1, 1 - slot, r0)
            normalise_rows(r0)
            return carry
        lax.fori_loop(0, n_chunks, body, 0)

    @pl.when(i == last)
    def _():
        def body(c, carry):
            normalise_rows(pl.multiple_of(c * chunk, chunk))
            return carry
        lax.fori_loop(0, n_chunks, body, 0)


def _combine_ln(pos, xn, wcol, gate, ln_g, ln_b, ys, *, seq, alpha):
    N, D = xn.shape
    D2 = ys.shape[1]
    tm = _pick(seq, 256)
    per = seq // tm
    grid_spec = pltpu.PrefetchScalarGridSpec(
        num_scalar_prefetch=1,
        grid=(N // tm,),
        in_specs=[pl.BlockSpec((tm, D), lambda i, pos: (i, 0)),
                  pl.BlockSpec((tm, LANES), lambda i, pos: (i, 0)),
                  pl.BlockSpec((1, 1, D), lambda i, pos: (i // per, 0, 0)),
                  pl.BlockSpec((1, D), lambda i, pos: (0, 0)),
                  pl.BlockSpec((1, D), lambda i, pos: (0, 0)),
                  pl.BlockSpec(memory_space=pl.ANY)],
        out_specs=pl.BlockSpec((tm, D), lambda i, pos: (i, 0)),
        scratch_shapes=[pltpu.VMEM((2, TOP_K, tm, D2), U32), pltpu.SemaphoreType.DMA((2,))],
    )
    return pl.pallas_call(
        functools.partial(_combine_ln_kernel, tm=tm, n_tok=N, alpha=alpha),
        grid_spec=grid_spec,
        out_shape=jax.ShapeDtypeStruct((N, D), F32),
        compiler_params=_params("arbitrary"),
        name="moe_combine_layernorm",
    )(pos, xn, wcol, gate, ln_g, ln_b, ys)


def _moe_sublayer(xn, hp, scale, shift, gate, ln_g, ln_b, w_group, b_group, w_router, b_router,
                  w_gate, w_up, w_down, layer, *, seq, alpha, tile):
    N, D = xn.shape
    G, _, E = w_router.shape
    n_exp = G * E
    w_t = jnp.concatenate([w_group.T, jnp.zeros((SUBLANES - G, D), F32),
                           w_router.transpose(0, 2, 1).reshape(n_exp, D)], axis=0)
    b_col = jnp.concatenate([b_group, jnp.zeros((SUBLANES - G,), F32), b_router.reshape(n_exp)])
    b_col = jnp.broadcast_to(b_col[:, None], (SUBLANES + n_exp, LANES))
    ids, wcol, counts = _router(xn, scale, shift, w_t, b_col, seq=seq, groups=G, experts=E)

    cnt = counts[:, 0].astype(I32)
    tiles_per = (cnt + tile - 1) // tile
    tile_end = jnp.cumsum(tiles_per)
    offs = (tile_end - tiles_per) * tile
    pos = jnp.concatenate([offs[ids[0]] + ids[2], offs[ids[1]] + ids[3]])
    n_tiles = (TOP_K * N) // tile + n_exp
    n_used = tile_end[n_exp - 1]
    jt = jnp.arange(n_tiles, dtype=I32)
    last = jnp.maximum(n_used - 1, 0)
    tile_block = jnp.minimum(jt, last)
    tile_expert = jnp.sum((tile_end[None, :] <= tile_block[:, None]).astype(I32), axis=1)
    tile_valid = (jt < n_used).astype(I32)

    xs = _dispatch(pos, hp, n_tiles * tile)
    ys = _experts(tile_expert + layer * n_exp, tile_block, tile_valid, xs, w_gate, w_up, w_down, tile=tile)
    return _combine_ln(pos, xn, wcol, gate, ln_g, ln_b, ys, seq=seq, alpha=alpha)


def _rope_tables(seq):
    inv_freq = jnp.power(ROPE_THETA, -jnp.arange(0, HEAD_DIM, 2, dtype=F32) / HEAD_DIM)
    ang = jnp.arange(seq, dtype=F32)[:, None] * inv_freq[None, :]
    cos, sin = jnp.cos(ang), jnp.sin(ang)
    return jnp.concatenate([cos, cos], axis=1), jnp.concatenate([-sin, sin], axis=1)


def kernel(x, c, mix_mod_w, mix_mod_b, mix_ln_g, mix_ln_b, a_w_in, a_lam_q1, a_lam_k1, a_lam_q2, a_lam_k2, a_subln_w, a_w_out, b_w_in, b_forget_bias, b_w_out, ffn_mod_w, ffn_mod_b, ffn_ln_g, ffn_ln_b, moe_w_group, moe_b_group, moe_w_router, moe_b_router, moe_w_gate, moe_w_up, moe_w_down):
    B, S, D = x.shape
    depth = mix_mod_w.shape[0]
    N = B * S
    alpha = (2.0 * depth) ** 0.25
    da_heads = D // (2 * HEAD_DIM)
    fx_heads = D // HEAD_DIM
    G, E, _, F = moe_w_gate.shape[1:]
    assert B <= SUBLANES and fx_heads <= GATE_LO_LANE and G <= SUBLANES

    c_pad = jnp.zeros((SUBLANES, D), F32).at[:B].set(c)
    mix_mod = _modulation(c_pad, mix_mod_w, mix_mod_b)[:, :B]
    ffn_mod = _modulation(c_pad, ffn_mod_w, ffn_mod_b)[:, :B]

    def split(m):
        return tuple(m[:, None, k * D:(k + 1) * D] for k in range(3))

    rope_tabs = _rope_tables(S)
    w_gate_all = moe_w_gate.reshape(depth * G * E, D, F)
    w_up_all = moe_w_up.reshape(depth * G * E, D, F)
    w_down_all = moe_w_down.reshape(depth * G * E, F, D)

    x2d = x.reshape(N, D)
    for i in range(depth):
        j = i // 2
        shift, scale, gate = split(mix_mod[i])
        shift2, scale2, gate2 = split(ffn_mod[i])
        if i % 2 == 0:
            qk_width = 2 * da_heads * HEAD_DIM
            proj = _inproj(x2d, scale, shift, a_w_in[j].astype(BF16), seq=S, q_width=qk_width,
                           rope_tabs=rope_tabs, rope_width=2 * qk_width)
            lam_vecs = jnp.stack([a_lam_q1[j], a_lam_k1[j], a_lam_q2[j], a_lam_k2[j]])
            lam_init = 0.8 - 0.6 * math.exp(-0.3 * i)
            o = _diff_attention(proj, lam_vecs, a_subln_w[j][None, :], batch=B, seq=S, heads=da_heads,
                                lam_init=lam_init)
            w_out = a_w_out[j]
        else:
            width = fx_heads * HEAD_DIM
            w_in = b_w_in[j]
            wf_hi = w_in[:, 3 * width:].astype(BF16)
            wf_lo = (w_in[:, 3 * width:] - wf_hi.astype(F32)).astype(BF16)
            w_f = jnp.zeros((2, D, LANES), BF16)
            w_f = w_f.at[:, :, :fx_heads].set(wf_hi).at[0, :, GATE_LO_LANE:GATE_LO_LANE + fx_heads].set(wf_lo)
            proj, f_logit = _inproj(x2d, scale, shift, w_in[:, :3 * width].astype(BF16), seq=S,
                                    q_width=width, w_gate=w_f)
            bias = jnp.zeros((1, LANES), F32).at[0, :fx_heads].set(b_forget_bias[j])
            cum = _gate_cumsum(f_logit, bias, batch=B, seq=S)
            o = _fox_attention(proj, cum, batch=B, seq=S, heads=fx_heads)
            w_out = b_w_out[j]
        xn, hp = _outproj_ln(o, w_out.astype(BF16), x2d, gate, mix_ln_g[i][None, :], mix_ln_b[i][None, :],
                             scale2, shift2, seq=S, alpha=alpha)
        x2d = _moe_sublayer(xn, hp, scale2, shift2, gate2, ffn_ln_g[i][None, :], ffn_ln_b[i][None, :],
                            moe_w_group[i], moe_b_group[i], moe_w_router[i], moe_b_router[i],
                            w_gate_all, w_up_all, w_down_all, i, seq=S, alpha=alpha, tile=256)
    return x2d.reshape(B, S, D)
```

```python
import functools
import math

import jax
import jax.numpy as jnp
from jax import lax
from jax.experimental import pallas as pl
from jax.experimental.pallas import tpu as pltpu

F32 = jnp.float32
BF16 = jnp.bfloat16
U32 = jnp.uint32
I32 = jnp.int32
HIGHEST = lax.Precision.HIGHEST

HEAD_DIM = 128
LANES = 128
SUBLANES = 8
MXU_COLS = 256
GATE_LO_LANE = 64
ROW_UNROLL = 8
ROPE_THETA = 10000.0
LN_EPS = 1e-5
RMS_EPS = 1e-5
TOP_K = 2
VMEM_LIMIT = 56 * 1024 * 1024

NT_DIMS = (((1,), (1,)), ((), ()))
LOG2E = math.log2(math.e)


def _params(*sem):
    return pltpu.CompilerParams(dimension_semantics=sem, vmem_limit_bytes=VMEM_LIMIT)


def _pick(total, want):
    t = min(total, want)
    assert total % t == 0, (total, want)
    return t


def _mod_kernel(c_ref, w_ref, b_ref, o_ref):
    c = c_ref[...]
    act = c * jax.nn.sigmoid(c)
    o_ref[0] = jnp.dot(act, w_ref[0], preferred_element_type=F32, precision=HIGHEST) + b_ref[0]


def _modulation(c_pad, w, b):
    L, D, D3 = w.shape
    tn = _pick(D3, 768)
    return pl.pallas_call(
        _mod_kernel,
        grid=(L, D3 // tn),
        in_specs=[pl.BlockSpec((SUBLANES, D), lambda l, j: (0, 0)),
                  pl.BlockSpec((1, D, tn), lambda l, j: (l, 0, j)),
                  pl.BlockSpec((1, 1, tn), lambda l, j: (l, 0, j))],
        out_specs=pl.BlockSpec((1, SUBLANES, tn), lambda l, j: (l, 0, j)),
        out_shape=jax.ShapeDtypeStruct((L, SUBLANES, D3), F32),
        compiler_params=_params("arbitrary", "arbitrary"),
        name="adaln_modulation",
    )(c_pad, w, b.reshape(L, 1, D3))


def _inproj_kernel(*refs, rope, has_gate, n_q_tiles, n_rope_tiles, q_scale, tn):
    it = iter(refs)
    x_ref, sc_ref, sh_ref, w_ref = next(it), next(it), next(it), next(it)
    cos_ref = sin_ref = wf_ref = f_ref = None
    if rope:
        cos_ref, sin_ref = next(it), next(it)
    if has_gate:
        wf_ref = next(it)
    o_ref = next(it)
    if has_gate:
        f_ref = next(it)
    h_ref = next(it)
    j = pl.program_id(1)

    @pl.when(j == 0)
    def _():
        h = x_ref[...] * (1.0 + sc_ref[0]) + sh_ref[0]
        h_hi = h.astype(BF16)
        h_ref[...] = h_hi
        if has_gate:
            h_lo = (h - h_hi.astype(F32)).astype(BF16)
            f = jnp.dot(h_hi, wf_ref[0], preferred_element_type=F32)
            f = f + pltpu.roll(f, LANES - GATE_LO_LANE, 1) + jnp.dot(h_lo, wf_ref[1], preferred_element_type=F32)
            f_ref[...] = f

    mult = jnp.where(j < n_q_tiles, jnp.float32(q_scale), jnp.float32(1.0))
    if rope:
        rotary = j < n_rope_tiles
        cos = jnp.where(rotary, cos_ref[...], 1.0) * mult
        sin = jnp.where(rotary, sin_ref[...], 0.0) * mult
    chunk = min(tn, MXU_COLS)
    for c in range(tn // chunk):
        o = jnp.dot(h_ref[...], w_ref[:, c * chunk:(c + 1) * chunk], preferred_element_type=F32)
        if rope:
            for k in range(chunk // HEAD_DIM):
                blk = o[:, k * HEAD_DIM:(k + 1) * HEAD_DIM]
                rot = pltpu.roll(blk, HEAD_DIM // 2, 1)
                col = c * chunk + k * HEAD_DIM
                o_ref[:, col:col + HEAD_DIM] = (blk * cos + rot * sin).astype(BF16)
        else:
            o_ref[:, c * chunk:(c + 1) * chunk] = (o * mult).astype(BF16)


def _inproj(x2d, scale, shift, w_bf, *, seq, q_width, rope_tabs=None, rope_width=0, w_gate=None):
    N, D = x2d.shape
    W = w_bf.shape[1]
    tm = _pick(seq, 1024)
    tn = _pick(q_width, 1024)
    assert W % tn == 0 and rope_width % tn == 0
    tiles_per_seq = seq // tm
    rope = rope_tabs is not None
    has_gate = w_gate is not None
    in_specs = [pl.BlockSpec((tm, D), lambda i, j: (i, 0)),
                pl.BlockSpec((1, 1, D), lambda i, j: (i // tiles_per_seq, 0, 0)),
                pl.BlockSpec((1, 1, D), lambda i, j: (i // tiles_per_seq, 0, 0)),
                pl.BlockSpec((D, tn), lambda i, j: (0, j))]
    args = [x2d, scale, shift, w_bf]
    if rope:
        tab_spec = pl.BlockSpec((tm, HEAD_DIM), lambda i, j: (i % tiles_per_seq, 0))
        in_specs += [tab_spec, tab_spec]
        args += list(rope_tabs)
    out_specs = [pl.BlockSpec((tm, tn), lambda i, j: (i, j))]
    out_shape = [jax.ShapeDtypeStruct((N, W), BF16)]
    if has_gate:
        in_specs.append(pl.BlockSpec((2, D, LANES), lambda i, j: (0, 0, 0)))
        args.append(w_gate)
        out_specs.append(pl.BlockSpec((tm, LANES), lambda i, j: (i, 0)))
        out_shape.append(jax.ShapeDtypeStruct((N, LANES), F32))
    kern = functools.partial(_inproj_kernel, rope=rope, has_gate=has_gate, n_q_tiles=q_width // tn,
                             n_rope_tiles=rope_width // tn, q_scale=HEAD_DIM ** -0.5 * LOG2E, tn=tn)
    outs = pl.pallas_call(
        kern,
        grid=(N // tm, W // tn),
        in_specs=in_specs,
        out_specs=out_specs,
        out_shape=out_shape,
        scratch_shapes=[pltpu.VMEM((tm, D), BF16)],
        compiler_params=_params("arbitrary", "arbitrary"),
        name="modulate_inproj",
    )(*args)
    return tuple(outs) if has_gate else outs[0]


def _gate_cumsum_kernel(f_ref, b_ref, o_ref, carry_ref, *, tc):
    @pl.when(pl.program_id(1) == 0)
    def _():
        carry_ref[...] = jnp.zeros_like(carry_ref)

    z = f_ref[...] + b_ref[...]
    log_f = jnp.minimum(z, 0.0) - jnp.log1p(jnp.exp(-jnp.abs(z)))
    row = lax.broadcasted_iota(I32, (tc, tc), 0)
    col = lax.broadcasted_iota(I32, (tc, tc), 1)
    tri = jnp.where(row >= col, 1.0, 0.0).astype(F32)
    cum = jnp.dot(tri, log_f, preferred_element_type=F32, precision=HIGHEST) + carry_ref[0:1, :]
    o_ref[...] = cum
    carry_ref[...] = jnp.broadcast_to(cum[tc - 1:tc, :], carry_ref.shape)


def _gate_cumsum(f_logit, bias_pad, *, batch, seq):
    N = f_logit.shape[0]
    tc = _pick(seq, 512)
    per = seq // tc
    return pl.pallas_call(
        functools.partial(_gate_cumsum_kernel, tc=tc),
        grid=(batch, per),
        in_specs=[pl.BlockSpec((tc, LANES), lambda b, i: (b * per + i, 0)),
                  pl.BlockSpec((1, LANES), lambda b, i: (0, 0))],
        out_specs=pl.BlockSpec((tc, LANES), lambda b, i: (b * per + i, 0)),
        out_shape=jax.ShapeDtypeStruct((N, LANES), F32),
        scratch_shapes=[pltpu.VMEM((SUBLANES, LANES), F32)],
        compiler_params=_params("arbitrary", "arbitrary"),
        name="forget_gate_cumsum",
    )(f_logit, bias_pad)


def _flash_sweep(qi, t, n_streams, get_qk, get_vt, s_ref, stats, dv):
    m_ref, l_ref, acc_ref = stats

    def scores(j, slot):
        for a in range(n_streams):
            q, k = get_qk(j, a)
            s_ref[slot, a] = lax.dot_general(k, q, NT_DIMS, preferred_element_type=F32)

    def consume(j, slot, mask):
        probs, alphas = [], []
        for a in range(n_streams):
            r = slice(SUBLANES * a, SUBLANES * a + 1)
            s = s_ref[slot, a]
            if mask is not None:
                s = jnp.where(mask, s, -jnp.inf)
            m_old = m_ref[r, :]
            m_new = jnp.maximum(m_old, jnp.max(s, axis=0, keepdims=True))
            alpha = jnp.exp2(m_old - m_new)
            p = jnp.exp2(s - m_new)
            l_ref[r, :] = alpha * l_ref[r, :] + jnp.sum(p, axis=0, keepdims=True)
            m_ref[r, :] = m_new
            probs.append(p.astype(BF16))
            alphas.append(alpha)
        for a in range(n_streams):
            rows = slice(a * dv, (a + 1) * dv)
            acc_ref[rows, :] = (alphas[a] * acc_ref[rows, :]
                                + jnp.dot(get_vt(j, a), probs[a], preferred_element_type=F32))

    _flash_reset(stats)
    scores(0, 0)

    def body(jj, c):
        j = 2 * jj
        scores(j + 1, 1)
        consume(j, 0, None)
        scores(j + 2, 0)
        consume(j + 1, 1, None)
        return c

    pairs = qi // 2
    lax.fori_loop(0, pairs, body, 0)

    @pl.when(qi % 2 == 0)
    def _():
        consume(qi, 0, _causal_mask_t(t))

    @pl.when(qi % 2 == 1)
    def _():
        scores(qi, 1)
        consume(qi - 1, 0, None)
        consume(qi, 1, _causal_mask_t(t))


def _flash_reset(stats):
    m_ref, l_ref, acc_ref = stats
    m_ref[...] = jnp.full(m_ref.shape, -jnp.inf, F32)
    l_ref[...] = jnp.zeros(l_ref.shape, F32)
    acc_ref[...] = jnp.zeros(acc_ref.shape, F32)


def _flash_result(stats, a, dv):
    _, l_ref, acc_ref = stats
    return acc_ref[a * dv:(a + 1) * dv, :] * (1.0 / l_ref[SUBLANES * a:SUBLANES * a + 1, :])


def _causal_mask_t(t):
    return lax.broadcasted_iota(I32, (t, t), 1) >= lax.broadcasted_iota(I32, (t, t), 0)


def _store_transposed_blocks(v_ref, vt_ref, t):
    for jb in range(vt_ref.shape[0]):
        vt_ref[jb] = v_ref[jb * t:(jb + 1) * t, :].astype(F32).T.astype(BF16)


FOX_PACK = 4
N_SPLIT = 3


def _fox_attn_kernel(q_ref, k_ref, v_ref, cum_ref, o_ref, qx_ref, kx_ref, vt_ref, s_ref, m_ref, l_ref, acc_ref,
                     *, t, seq):
    hp = pl.program_id(1)
    qi = pl.program_id(2)
    d = HEAD_DIM
    stats = (m_ref, l_ref, acc_ref)

    @pl.when(qi == 0)
    def _():
        _store_transposed_blocks(v_ref, vt_ref, t)
        lane = lax.broadcasted_iota(I32, (seq, LANES), 1)
        for a in range(FOX_PACK):
            c = jnp.sum(jnp.where(lane == hp * FOX_PACK + a, cum_ref[...], 0.0), axis=1, keepdims=True)
            rest = jnp.broadcast_to(c * LOG2E, (seq, LANES))
            qx = jnp.where(lane < 2 * N_SPLIT, 1.0, 0.0)
            kx = qx
            for i in range(N_SPLIT):
                piece = rest.astype(BF16).astype(F32)
                rest = rest - piece
                qx = jnp.where(lane == i, piece, qx)
                kx = jnp.where(lane == N_SPLIT + i, -piece, kx)
            qx_ref[a] = qx.astype(BF16)
            kx_ref[a] = kx.astype(BF16)

    q0 = pl.multiple_of(qi * t, t)

    def get_qk(j, a):
        k0 = pl.multiple_of(j * t, t)
        q = jnp.concatenate([q_ref[:, a * d:(a + 1) * d], qx_ref[a, pl.ds(q0, t), :]], axis=1)
        k = jnp.concatenate([k_ref[pl.ds(k0, t), a * d:(a + 1) * d], kx_ref[a, pl.ds(k0, t), :]], axis=1)
        return q, k

    def get_vt(j, a):
        return vt_ref[j, a * d:(a + 1) * d, :]

    _flash_sweep(qi, t, FOX_PACK, get_qk, get_vt, s_ref, stats, d)
    for a in range(FOX_PACK):
        o_ref[:, a * d:(a + 1) * d] = _flash_result(stats, a, d).T.astype(BF16)


def _fox_attention(proj, cum, *, batch, seq, heads):
    N = proj.shape[0]
    t = _pick(seq, 512)
    nq = seq // t
    w = FOX_PACK * HEAD_DIM
    assert heads % FOX_PACK == 0
    hb = heads // FOX_PACK
    return pl.pallas_call(
        functools.partial(_fox_attn_kernel, t=t, seq=seq),
        grid=(batch, hb, nq),
        in_specs=[pl.BlockSpec((t, w), lambda b, h, i: (b * nq + i, h)),
                  pl.BlockSpec((seq, w), lambda b, h, i: (b, hb + h)),
                  pl.BlockSpec((seq, w), lambda b, h, i: (b, 2 * hb + h)),
                  pl.BlockSpec((seq, LANES), lambda b, h, i: (b, 0))],
        out_specs=pl.BlockSpec((t, w), lambda b, h, i: (b * nq + i, h)),
        out_shape=jax.ShapeDtypeStruct((N, heads * HEAD_DIM), BF16),
        scratch_shapes=[pltpu.VMEM((FOX_PACK, seq, LANES), BF16),
                        pltpu.VMEM((FOX_PACK, seq, LANES), BF16),
                        pltpu.VMEM((seq // t, w, t), BF16),
                        pltpu.VMEM((2, FOX_PACK, t, t), F32),
                        pltpu.VMEM((FOX_PACK * SUBLANES, t), F32),
                        pltpu.VMEM((FOX_PACK * SUBLANES, t), F32),
                        pltpu.VMEM((w, t), F32)],
        compiler_params=_params("arbitrary", "arbitrary", "arbitrary"),
        name="forgetting_attention",
    )(proj, proj, proj, cum)


DIFF_PACK = 2


def _diff_attn_kernel(q_ref, k_ref, v_ref, lam_ref, subln_ref, o_ref, vt_ref, s_ref, m_ref, l_ref, acc_ref,
                      *, t, lam_init):
    qi = pl.program_id(2)
    d = HEAD_DIM
    dv = 2 * d
    stats = (m_ref, l_ref, acc_ref)
    n_streams = 2 * DIFF_PACK

    @pl.when(qi == 0)
    def _():
        _store_transposed_blocks(v_ref, vt_ref, t)

    def get_qk(j, a):
        k0 = pl.multiple_of(j * t, t)
        return q_ref[:, a * d:(a + 1) * d], k_ref[pl.ds(k0, t), a * d:(a + 1) * d]

    def get_vt(j, a):
        return vt_ref[j, (a // 2) * dv:(a // 2 + 1) * dv, :]

    _flash_sweep(qi, t, n_streams, get_qk, get_vt, s_ref, stats, dv)

    lv = lam_ref[...]
    lam = (jnp.exp(jnp.sum(lv[0:1] * lv[1:2], axis=1, keepdims=True))
           - jnp.exp(jnp.sum(lv[2:3] * lv[3:4], axis=1, keepdims=True)) + lam_init)
    for g in range(DIFF_PACK):
        o_t = _flash_result(stats, 2 * g, dv) - lam * _flash_result(stats, 2 * g + 1, dv)
        o_t = o_t * lax.rsqrt(jnp.mean(o_t * o_t, axis=0, keepdims=True) + RMS_EPS)
        o_ref[:, g * dv:(g + 1) * dv] = (o_t.T * subln_ref[...] * (1.0 - lam_init)).astype(BF16)


def _diff_attention(proj, lam_vecs, subln_w, *, batch, seq, heads, lam_init):
    N = proj.shape[0]
    t = _pick(seq, 512)
    nq = seq // t
    dv = 2 * HEAD_DIM
    w = DIFF_PACK * dv
    assert heads % DIFF_PACK == 0
    hb = heads // DIFF_PACK
    return pl.pallas_call(
        functools.partial(_diff_attn_kernel, t=t, lam_init=lam_init),
        grid=(batch, hb, nq),
        in_specs=[pl.BlockSpec((t, w), lambda b, h, i: (b * nq + i, h)),
                  pl.BlockSpec((seq, w), lambda b, h, i: (b, hb + h)),
                  pl.BlockSpec((seq, w), lambda b, h, i: (b, 2 * hb + h)),
                  pl.BlockSpec((4, HEAD_DIM), lambda b, h, i: (0, 0)),
                  pl.BlockSpec((1, dv), lambda b, h, i: (0, 0))],
        out_specs=pl.BlockSpec((t, w), lambda b, h, i: (b * nq + i, h)),
        out_shape=jax.ShapeDtypeStruct((N, heads * dv), BF16),
        scratch_shapes=[pltpu.VMEM((seq // t, w, t), BF16),
                        pltpu.VMEM((2, 2 * DIFF_PACK, t, t), F32),
                        pltpu.VMEM((2 * DIFF_PACK * SUBLANES, t), F32),
                        pltpu.VMEM((2 * DIFF_PACK * SUBLANES, t), F32),
                        pltpu.VMEM((2 * DIFF_PACK * dv, t), F32)],
        compiler_params=_params("arbitrary", "arbitrary", "arbitrary"),
        name="differential_attention",
    )(proj, proj, proj, lam_vecs, subln_w)


def _pack_halves(y):
    w = y.shape[1] // 2
    lo = lax.bitcast_convert_type(y[:, :w].astype(BF16).astype(F32), U32)
    hi = lax.bitcast_convert_type(y[:, w:].astype(BF16).astype(F32), U32)
    return (lo >> 16) | (hi & jnp.uint32(0xFFFF0000))


def _unpack_halves(u):
    lo = lax.bitcast_convert_type(u << 16, F32)
    hi = lax.bitcast_convert_type(u & jnp.uint32(0xFFFF0000), F32)
    return lo, hi


def _layer_norm(z, g, b):
    mu = jnp.mean(z, axis=1, keepdims=True)
    zc = z - mu
    var = jnp.mean(zc * zc, axis=1, keepdims=True)
    return zc * lax.rsqrt(var + LN_EPS) * g + b


def _outproj_ln_kernel(o_ref, w_ref, x_ref, gate_ref, g_ref, b_ref, sc_ref, sh_ref, xo_ref, hp_ref, *, alpha):
    half = o_ref.shape[0] // 2
    for r in range(2):
        rows = slice(r * half, (r + 1) * half)
        y = jnp.dot(o_ref[rows, :], w_ref[...], preferred_element_type=F32)
        z = alpha * x_ref[rows, :] + (1.0 + gate_ref[0]) * y
        xn = _layer_norm(z, g_ref[...], b_ref[...])
        xo_ref[rows, :] = xn
        hp_ref[rows, :] = _pack_halves(xn * (1.0 + sc_ref[0]) + sh_ref[0])


def _outproj_ln(o, w_bf, x2d, gate, ln_g, ln_b, scale2, shift2, *, seq, alpha):
    N, D = x2d.shape
    K = o.shape[1]
    tm = _pick(seq, 512)
    per = seq // tm
    bspec = pl.BlockSpec((1, 1, D), lambda i: (i // per, 0, 0))
    vspec = pl.BlockSpec((1, D), lambda i: (0, 0))
    return pl.pallas_call(
        functools.partial(_outproj_ln_kernel, alpha=alpha),
        grid=(N // tm,),
        in_specs=[pl.BlockSpec((tm, K), lambda i: (i, 0)),
                  pl.BlockSpec((K, D), lambda i: (0, 0)),
                  pl.BlockSpec((tm, D), lambda i: (i, 0)),
                  bspec, vspec, vspec, bspec, bspec],
        out_specs=[pl.BlockSpec((tm, D), lambda i: (i, 0)),
                   pl.BlockSpec((tm, D // 2), lambda i: (i, 0))],
        out_shape=[jax.ShapeDtypeStruct((N, D), F32), jax.ShapeDtypeStruct((N, D // 2), U32)],
        compiler_params=_params("arbitrary"),
        name="outproj_layernorm",
    )(o, w_bf, x2d, gate, ln_g, ln_b, scale2, shift2)


def _router_kernel(x_ref, sc_ref, sh_ref, w_ref, b_ref, ids_ref, wcol_ref, cnt_ref, carry_ref,
                   *, tm, groups, experts):
    @pl.when(pl.program_id(0) == 0)
    def _():
        carry_ref[...] = jnp.zeros_like(carry_ref)

    n_exp = groups * experts
    h = x_ref[...] * (1.0 + sc_ref[0]) + sh_ref[0]
    logits = lax.dot_general(w_ref[...], h, NT_DIMS, preferred_element_type=F32, precision=HIGHEST)
    logits = logits + b_ref[:, 0:1]

    gl = logits[0:groups]
    gmax = jnp.max(gl, axis=0, keepdims=True)
    gid = lax.broadcasted_iota(I32, gl.shape, 0).astype(F32)
    gidx = jnp.min(jnp.where(gl == gmax, gid, float(groups)), axis=0, keepdims=True)
    g_w = 1.0 / jnp.sum(jnp.exp(gl - gmax), axis=0, keepdims=True)

    el = logits[SUBLANES:SUBLANES + experts]
    for g in range(1, groups):
        el = jnp.where(gidx == float(g), logits[SUBLANES + g * experts:SUBLANES + (g + 1) * experts], el)
    eid = lax.broadcasted_iota(I32, el.shape, 0).astype(F32)
    v1 = jnp.max(el, axis=0, keepdims=True)
    i1 = jnp.min(jnp.where(el == v1, eid, float(experts)), axis=0, keepdims=True)
    el2 = jnp.where(eid == i1, -jnp.inf, el)
    v2 = jnp.max(el2, axis=0, keepdims=True)
    i2 = jnp.min(jnp.where(el2 == v2, eid, float(experts)), axis=0, keepdims=True)
    t = jnp.exp(v2 - v1)
    p1 = 1.0 / (1.0 + t)
    w1 = g_w * p1
    w2 = g_w * (t * p1)
    e1 = gidx * float(experts) + i1
    e2 = gidx * float(experts) + i2

    xid = lax.broadcasted_iota(I32, (n_exp, tm), 0).astype(F32)
    oh1 = xid == e1
    oh2 = xid == e2
    cnt = jnp.where(oh1, 1.0, 0.0) + jnp.where(oh2, 1.0, 0.0)
    before = (lax.broadcasted_iota(I32, (tm, tm), 0) < lax.broadcasted_iota(I32, (tm, tm), 1))
    prefix = jnp.dot(cnt.astype(BF16), jnp.where(before, 1.0, 0.0).astype(BF16), preferred_element_type=F32)
    base = carry_ref[:, 0:1] + prefix
    r1 = jnp.sum(jnp.where(oh1, base, 0.0), axis=0, keepdims=True)
    r2 = jnp.sum(jnp.where(oh2, base, 0.0), axis=0, keepdims=True)
    carry_ref[...] = carry_ref[...] + jnp.sum(cnt, axis=1, keepdims=True)
    cnt_ref[...] = carry_ref[...]

    ids_ref[...] = jnp.zeros_like(ids_ref)
    ids_ref[0:1, :] = e1.astype(I32)
    ids_ref[1:2, :] = e2.astype(I32)
    ids_ref[2:3, :] = r1.astype(I32)
    ids_ref[3:4, :] = r2.astype(I32)
    rid = lax.broadcasted_iota(I32, (LANES, tm), 0)
    wrows = jnp.where(rid == 0, w1, jnp.where(rid == 1, w2, 0.0))
    wcol_ref[...] = wrows.T


def _router(xn, scale, shift, w_t, b_col, *, seq, groups, experts):
    N, D = xn.shape
    tm = _pick(seq, 512)
    per = seq // tm
    rows = w_t.shape[0]
    n_exp = groups * experts
    bspec = pl.BlockSpec((1, 1, D), lambda i: (i // per, 0, 0))
    return pl.pallas_call(
        functools.partial(_router_kernel, tm=tm, groups=groups, experts=experts),
        grid=(N // tm,),
        in_specs=[pl.BlockSpec((tm, D), lambda i: (i, 0)), bspec, bspec,
                  pl.BlockSpec((rows, D), lambda i: (0, 0)),
                  pl.BlockSpec((rows, LANES), lambda i: (0, 0))],
        out_specs=[pl.BlockSpec((SUBLANES, tm), lambda i: (0, i)),
                   pl.BlockSpec((tm, LANES), lambda i: (i, 0)),
                   pl.BlockSpec((n_exp, LANES), lambda i: (0, 0))],
        out_shape=[jax.ShapeDtypeStruct((SUBLANES, N), I32),
                   jax.ShapeDtypeStruct((N, LANES), F32),
                   jax.ShapeDtypeStruct((n_exp, LANES), F32)],
        scratch_shapes=[pltpu.VMEM((n_exp, LANES), F32)],
        compiler_params=_params("arbitrary"),
        name="moe_router",
    )(xn, scale, shift, w_t, b_col)


def _dispatch_kernel(pos_ref, ztile_ref, hp_ref, xs_ref, zbuf, sem, zsem, *, tm, n_tok, tile):
    base = pl.program_id(0) * tm

    @pl.when(pl.program_id(0) == 0)
    def _():
        zbuf[...] = jnp.zeros(zbuf.shape, U32)

        def zero_tile(jt):
            row = pl.multiple_of(jt * tile, tile)
            cp = pltpu.make_async_copy(zbuf, xs_ref.at[pl.ds(row, tile), :], zsem)
            cp.start()
            cp.wait()

        def zero_last(e, c):
            zero_tile(ztile_ref[e])
            return c

        def zero_unused(jt, c):
            zero_tile(jt)
            return c

        n_exp = ztile_ref.shape[0] - 1
        lax.fori_loop(0, n_exp, zero_last, 0)
        lax.fori_loop(ztile_ref[n_exp], xs_ref.shape[0] // tile, zero_unused, 0)

    def issue(g, c):
        for u in range(ROW_UNROLL):
            r = g * ROW_UNROLL + u
            for k in range(TOP_K):
                pltpu.make_async_copy(hp_ref.at[pl.ds(r, 1), :],
                                      xs_ref.at[pl.ds(pos_ref[k * n_tok + base + r], 1), :], sem).start()
        return c

    lax.fori_loop(0, tm // ROW_UNROLL, issue, 0)
    for _ in range(TOP_K):
        pltpu.make_async_copy(hp_ref, xs_ref.at[pl.ds(0, tm), :], sem).wait()


def _dispatch(pos, last_tile, hp, n_slots, tile):
    N, D2 = hp.shape
    tm = _pick(N, 256)
    grid_spec = pltpu.PrefetchScalarGridSpec(
        num_scalar_prefetch=2,
        grid=(N // tm,),
        in_specs=[pl.BlockSpec((tm, D2), lambda i, pos, zt: (i, 0))],
        out_specs=pl.BlockSpec(memory_space=pl.ANY),
        scratch_shapes=[pltpu.VMEM((tile, D2), U32), pltpu.SemaphoreType.DMA(()), pltpu.SemaphoreType.DMA(())],
    )
    return pl.pallas_call(
        functools.partial(_dispatch_kernel, tm=tm, n_tok=N, tile=tile),
        grid_spec=grid_spec,
        out_shape=jax.ShapeDtypeStruct((n_slots, D2), U32),
        compiler_params=_params("arbitrary"),
        name="moe_dispatch",
    )(pos, last_tile, hp)


def _expert_kernel(te_ref, tb_ref, tv_ref, xs_ref, wg_ref, wu_ref, wd_ref, ys_ref, wg_bf, wu_bf, wd_bf):
    j = pl.program_id(0)
    prev = te_ref[jnp.maximum(j - 1, 0)]

    @pl.when(jnp.logical_or(j == 0, te_ref[j] != prev))
    def _():
        wg_bf[...] = wg_ref[0].astype(BF16)
        wu_bf[...] = wu_ref[0].astype(BF16)
        wd_bf[...] = wd_ref[0].astype(BF16)

    @pl.when(tv_ref[j] == 1)
    def _():
        lo, hi = _unpack_halves(xs_ref[...])
        x = jnp.concatenate([lo.astype(BF16), hi.astype(BF16)], axis=1)
        a = jnp.dot(x, wg_bf[...], preferred_element_type=F32)
        u = jnp.dot(x, wu_bf[...], preferred_element_type=F32)
        hid = (a * jax.nn.sigmoid(a) * u).astype(BF16)
        ys_ref[...] = _pack_halves(jnp.dot(hid, wd_bf[...], preferred_element_type=F32))


def _experts(tile_expert, tile_block, tile_valid, xs, w_gate, w_up, w_down, *, tile):
    P, D2 = xs.shape
    _, D, F = w_gate.shape
    n_tiles = P // tile
    grid_spec = pltpu.PrefetchScalarGridSpec(
        num_scalar_prefetch=3,
        grid=(n_tiles,),
        in_specs=[pl.BlockSpec((tile, D2), lambda j, te, tb, tv: (tb[j], 0)),
                  pl.BlockSpec((1, D, F), lambda j, te, tb, tv: (te[j], 0, 0)),
                  pl.BlockSpec((1, D, F), lambda j, te, tb, tv: (te[j], 0, 0)),
                  pl.BlockSpec((1, F, D), lambda j, te, tb, tv: (te[j], 0, 0))],
        out_specs=pl.BlockSpec((tile, D2), lambda j, te, tb, tv: (tb[j], 0)),
        scratch_shapes=[pltpu.VMEM((D, F), BF16), pltpu.VMEM((D, F), BF16), pltpu.VMEM((F, D), BF16)],
    )
    return pl.pallas_call(
        _expert_kernel,
        grid_spec=grid_spec,
        out_shape=jax.ShapeDtypeStruct((P, D2), U32),
        input_output_aliases={3: 0},
        compiler_params=_params("arbitrary"),
        name="moe_experts",
    )(tile_expert, tile_block, tile_valid, xs, w_gate, w_up, w_down)


def _combine_ln_kernel(pos_ref, x_ref, wcol_ref, gate_ref, g_ref, b_ref, ys_ref, o_ref, buf, sem,
                       *, tm, n_tok, alpha):
    i = pl.program_id(0)
    last = pl.num_programs(0) - 1
    slot = i % 2
    chunk = ROW_UNROLL * 4
    n_chunks = tm // chunk

    def issue_rows(tile, s, r0):
        base = tile * tm
        for u in range(chunk):
            r = r0 + u
            for k in range(TOP_K):
                pltpu.make_async_copy(ys_ref.at[pl.ds(pos_ref[k * n_tok + base + r], 1), :],
                                      buf.at[s, k, pl.ds(r, 1), :], sem.at[s]).start()

    def normalise_rows(r0):
        rows = pl.ds(r0, chunk)
        w = wcol_ref[rows, :]
        lo1, hi1 = _unpack_halves(buf[slot, 0, rows, :])
        lo2, hi2 = _unpack_halves(buf[slot, 1, rows, :])
        w1 = w[:, 0:1]
        w2 = w[:, 1:2]
        y = jnp.concatenate([w1 * lo1 + w2 * lo2, w1 * hi1 + w2 * hi2], axis=1)
        z = alpha * x_ref[rows, :] + (1.0 + gate_ref[0]) * y
        o_ref[rows, :] = _layer_norm(z, g_ref[...], b_ref[...])

    @pl.when(i == 0)
    def _():
        def first(c, carry):
            issue_rows(0, 0, pl.multiple_of(c * chunk, chunk))
            return carry
        lax.fori_loop(0, n_chunks, first, 0)

    for k in range(TOP_K):
        pltpu.make_async_copy(ys_ref.at[pl.ds(0, tm), :], buf.at[slot, k], sem.at[slot]).wait()

    @pl.when(i < last)
    def _():
        def body(c, carry):
            r0 = pl.multiple_of(c * chunk, chunk)
            issue_rows(i + 1, 1 - slot, r0)
            normalise_rows(r0)
            return carry
        lax.fori_loop(0, n_chunks, body, 0)

    @pl.when(i == last)
    def _():
        def body(c, carry):
            normalise_rows(pl.multiple_of(c * chunk, chunk))
            return carry
        lax.fori_loop(0, n_chunks, body, 0)


def _combine_ln(pos, xn, wcol, gate, ln_g, ln_b, ys, *, seq, alpha):
    N, D = xn.shape
    D2 = ys.shape[1]
    tm = _pick(seq, 256)
    per = seq // tm
    grid_spec = pltpu.PrefetchScalarGridSpec(
        num_scalar_prefetch=1,
        grid=(N // tm,),
        in_specs=[pl.BlockSpec((tm, D), lambda i, pos: (i, 0)),
                  pl.BlockSpec((tm, LANES), lambda i, pos: (i, 0)),
                  pl.BlockSpec((1, 1, D), lambda i, pos: (i // per, 0, 0)),
                  pl.BlockSpec((1, D), lambda i, pos: (0, 0)),
                  pl.BlockSpec((1, D), lambda i, pos: (0, 0)),
                  pl.BlockSpec(memory_space=pl.ANY)],
        out_specs=pl.BlockSpec((tm, D), lambda i, pos: (i, 0)),
        scratch_shapes=[pltpu.VMEM((2, TOP_K, tm, D2), U32), pltpu.SemaphoreType.DMA((2,))],
    )
    return pl.pallas_call(
        functools.partial(_combine_ln_kernel, tm=tm, n_tok=N, alpha=alpha),
        grid_spec=grid_spec,
        out_shape=jax.ShapeDtypeStruct((N, D), F32),
        compiler_params=_params("arbitrary"),
        name="moe_combine_layernorm",
    )(pos, xn, wcol, gate, ln_g, ln_b, ys)


def _moe_sublayer(xn, hp, scale, shift, gate, ln_g, ln_b, w_group, b_group, w_router, b_router,
                  w_gate, w_up, w_down, layer, *, seq, alpha, tile):
    N, D = xn.shape
    G, _, E = w_router.shape
    n_exp = G * E
    w_t = jnp.concatenate([w_group.T, jnp.zeros((SUBLANES - G, D), F32),
                           w_router.transpose(0, 2, 1).reshape(n_exp, D)], axis=0)
    b_col = jnp.concatenate([b_group, jnp.zeros((SUBLANES - G,), F32), b_router.reshape(n_exp)])
    b_col = jnp.broadcast_to(b_col[:, None], (SUBLANES + n_exp, LANES))
    ids, wcol, counts = _router(xn, scale, shift, w_t, b_col, seq=seq, groups=G, experts=E)

    cnt = counts[:, 0].astype(I32)
    tiles_per = (cnt + tile - 1) // tile
    tile_end = jnp.cumsum(tiles_per)
    offs = (tile_end - tiles_per) * tile
    pos = jnp.concatenate([offs[ids[0]] + ids[2], offs[ids[1]] + ids[3]])
    n_tiles = (TOP_K * N) // tile + n_exp
    n_used = tile_end[n_exp - 1]
    jt = jnp.arange(n_tiles, dtype=I32)
    last = jnp.maximum(n_used - 1, 0)
    tile_block = jnp.minimum(jt, last)
    tile_expert = jnp.sum((tile_end[None, :] <= tile_block[:, None]).astype(I32), axis=1)
    tile_valid = (jt < n_used).astype(I32)

    last_tile = jnp.concatenate([jnp.maximum(tile_end - 1, 0), n_used[None]]).astype(I32)
    xs = _dispatch(pos, last_tile, hp, n_tiles * tile, tile)
    ys = _experts(tile_expert + layer * n_exp, tile_block, tile_valid, xs, w_gate, w_up, w_down, tile=tile)
    return _combine_ln(pos, xn, wcol, gate, ln_g, ln_b, ys, seq=seq, alpha=alpha)


def _rope_tables(seq):
    inv_freq = jnp.power(ROPE_THETA, -jnp.arange(0, HEAD_DIM, 2, dtype=F32) / HEAD_DIM)
    ang = jnp.arange(seq, dtype=F32)[:, None] * inv_freq[None, :]
    cos, sin = jnp.cos(ang), jnp.sin(ang)
    return jnp.concatenate([cos, cos], axis=1), jnp.concatenate([-sin, sin], axis=1)


def kernel(x, c, mix_mod_w, mix_mod_b, mix_ln_g, mix_ln_b, a_w_in, a_lam_q1, a_lam_k1, a_lam_q2, a_lam_k2, a_subln_w, a_w_out, b_w_in, b_forget_bias, b_w_out, ffn_mod_w, ffn_mod_b, ffn_ln_g, ffn_ln_b, moe_w_group, moe_b_group, moe_w_router, moe_b_router, moe_w_gate, moe_w_up, moe_w_down):
    B, S, D = x.shape
    depth = mix_mod_w.shape[0]
    N = B * S
    alpha = (2.0 * depth) ** 0.25
    da_heads = D // (2 * HEAD_DIM)
    fx_heads = D // HEAD_DIM
    G, E, _, F = moe_w_gate.shape[1:]
    assert B <= SUBLANES and fx_heads <= GATE_LO_LANE and G <= SUBLANES

    c_pad = jnp.zeros((SUBLANES, D), F32).at[:B].set(c)
    mix_mod = _modulation(c_pad, mix_mod_w, mix_mod_b)[:, :B]
    ffn_mod = _modulation(c_pad, ffn_mod_w, ffn_mod_b)[:, :B]

    def split(m):
        return tuple(m[:, None, k * D:(k + 1) * D] for k in range(3))

    rope_tabs = _rope_tables(S)
    w_gate_all = moe_w_gate.reshape(depth * G * E, D, F)
    w_up_all = moe_w_up.reshape(depth * G * E, D, F)
    w_down_all = moe_w_down.reshape(depth * G * E, F, D)

    x2d = x.reshape(N, D)
    for i in range(depth):
        j = i // 2
        shift, scale, gate = split(mix_mod[i])
        shift2, scale2, gate2 = split(ffn_mod[i])
        if i % 2 == 0:
            qk_width = 2 * da_heads * HEAD_DIM
            proj = _inproj(x2d, scale, shift, a_w_in[j].astype(BF16), seq=S, q_width=qk_width,
                           rope_tabs=rope_tabs, rope_width=2 * qk_width)
            lam_vecs = jnp.stack([a_lam_q1[j], a_lam_k1[j], a_lam_q2[j], a_lam_k2[j]])
            lam_init = 0.8 - 0.6 * math.exp(-0.3 * i)
            o = _diff_attention(proj, lam_vecs, a_subln_w[j][None, :], batch=B, seq=S, heads=da_heads,
                                lam_init=lam_init)
            w_out = a_w_out[j]
        else:
            width = fx_heads * HEAD_DIM
            w_in = b_w_in[j]
            wf_hi = w_in[:, 3 * width:].astype(BF16)
            wf_lo = (w_in[:, 3 * width:] - wf_hi.astype(F32)).astype(BF16)
            w_f = jnp.zeros((2, D, LANES), BF16)
            w_f = w_f.at[:, :, :fx_heads].set(wf_hi).at[0, :, GATE_LO_LANE:GATE_LO_LANE + fx_heads].set(wf_lo)
            proj, f_logit = _inproj(x2d, scale, shift, w_in[:, :3 * width].astype(BF16), seq=S,
                                    q_width=width, w_gate=w_f)
            bias = jnp.zeros((1, LANES), F32).at[0, :fx_heads].set(b_forget_bias[j])
            cum = _gate_cumsum(f_logit, bias, batch=B, seq=S)
            o = _fox_attention(proj, cum, batch=B, seq=S, heads=fx_heads)
            w_out = b_w_out[j]
        xn, hp = _outproj_ln(o, w_out.astype(BF16), x2d, gate, mix_ln_g[i][None, :], mix_ln_b[i][None, :],
                             scale2, shift2, seq=S, alpha=alpha)
        x2d = _moe_sublayer(xn, hp, scale2, shift2, gate2, ffn_ln_g[i][None, :], ffn_ln_b[i][None, :],
                            moe_w_group[i], moe_b_group[i], moe_w_router[i], moe_b_router[i],
                            w_gate_all, w_up_all, w_down_all, i, seq=S, alpha=alpha, tile=256)
    return x2d.reshape(B, S, D)
```
